```python
import numpy as np
import jax
import jax.numpy as jnp
from jax import lax

D_MODEL = 1024
BATCH = 16
SEQ = 4096
DEPTH = 2
DEC_BATCH = 8
DEC_SEQ = 8192
PAST_LEN = 128

GRID_W = 64
D_CONV = 256
CONV_K = 31
D_POOL = 256
POOL_WINDOWS = (2, 4, 8, 16)
N_POOL_GROUPS = len(POOL_WINDOWS)
POOL_GROUP = D_POOL // N_POOL_GROUPS
D_SC = 256
SC_K = 3
N_HEADS = 4
HEAD_DIM = 64
D_ATT = N_HEADS * HEAD_DIM
WIN_ROWS = 8
WIN_COLS = 16
Q_BLOCK_COLS = 16
KEY_BLOCK_COLS = 32
N_BRANCH = 4
D_FF = 2816
FFN_K = 3
EPS = 1e-6
NEG_INF = -1e30

OFF_A = 0
OFF_B = OFF_A + 2 * D_CONV
OFF_C = OFF_B + D_POOL
OFF_D = OFF_C + 3 * D_SC
OFF_G = OFF_D + 3 * D_ATT
D_IN = OFF_G + N_BRANCH * D_MODEL

kernel_name = 'hybrid_gated_parallel_encoder'


def rms_norm(x, g):
    xf = x.astype(jnp.float32)
    y = xf * lax.rsqrt(jnp.mean(xf * xf, axis=-1, keepdims=True) + EPS)
    return (y * g.astype(jnp.float32)).astype(x.dtype)


def layer_norm(x, g, b):
    xf = x.astype(jnp.float32)
    mu = jnp.mean(xf, axis=-1, keepdims=True)
    var = jnp.mean(jnp.square(xf - mu), axis=-1, keepdims=True)
    y = (xf - mu) * lax.rsqrt(var + EPS) * g.astype(jnp.float32) + b.astype(jnp.float32)
    return y.astype(x.dtype)


def depthwise_conv(x, w):
    k = w.shape[0]
    return lax.conv_general_dilated(
        x, w[:, None, :].astype(x.dtype), window_strides=(1,),
        padding=[(k // 2, k // 2)], dimension_numbers=('NWC', 'WIO', 'NWC'),
        feature_group_count=x.shape[-1])


def conformer_conv(za, conv_a_w, conv_a_b, ln_a_g, ln_a_b, w_out_a):
    a_val, a_gate = jnp.split(za, 2, axis=-1)
    a = a_val * jax.nn.sigmoid(a_gate)
    a = depthwise_conv(a, conv_a_w) + conv_a_b
    a = jax.nn.silu(layer_norm(a, ln_a_g, ln_a_b))
    return a @ w_out_a


def multiscale_pool(xb, pool_w, pool_scale, w_out_b):
    b, t, _ = xb.shape
    cs = jnp.concatenate([jnp.zeros((b, 1, D_POOL), jnp.float32),
                          lax.cumsum(xb.astype(jnp.float32), axis=1)], axis=1)
    pos = np.arange(t)
    outs = []
    for gi, w in enumerate(POOL_WINDOWS):
        lo = np.clip(pos - w // 2, 0, t - 1).astype(np.int32)
        hi = np.clip(pos + w // 2 - 1, 0, t - 1).astype(np.int32)
        cnt = (hi - lo + 1).astype(np.float32)
        sl = slice(gi * POOL_GROUP, (gi + 1) * POOL_GROUP)
        c = cs[..., sl]
        mean = (jnp.take(c, hi + 1, axis=1) - jnp.take(c, lo, axis=1)) / cnt[:, None]
        outs.append(mean - xb[..., sl].astype(jnp.float32))
    p = jnp.stack(outs, axis=2).astype(xb.dtype)
    p = jnp.einsum('btgc,gce->btge', p, pool_w).reshape(b, t, D_POOL) * pool_scale
    return p @ w_out_b


def short_gated_conv(zc, sc_w, w_out_c):
    gb, gc, xv = jnp.split(zc, 3, axis=-1)
    return (gb * depthwise_conv(gc * xv, sc_w)) @ w_out_c


def neighbourhood_attention(q, k, v, rpb):
    b, t = q.shape[:2]
    rows = t // GRID_W
    kr = min(WIN_ROWS, rows)
    r = np.arange(rows)
    rs = np.clip(r - kr // 2, 0, rows - kr)
    row_idx = (rs[:, None] + np.arange(kr)[None]).astype(np.int32)
    row_off = (row_idx - r[:, None] + (WIN_ROWS - 1)).astype(np.int32)
    n_cb = GRID_W // Q_BLOCK_COLS
    qcol = np.arange(GRID_W).reshape(n_cb, Q_BLOCK_COLS)
    cstart = np.clip(qcol - WIN_COLS // 2, 0, GRID_W - WIN_COLS)
    cb = np.clip(np.arange(n_cb) * Q_BLOCK_COLS - WIN_COLS // 2, 0, GRID_W - KEY_BLOCK_COLS)
    col_idx = (cb[:, None] + np.arange(KEY_BLOCK_COLS)[None]).astype(np.int32)
    kcol = col_idx[:, None, :]
    col_mask = (kcol >= cstart[..., None]) & (kcol < cstart[..., None] + WIN_COLS)
    col_off = np.clip(kcol - qcol[..., None] + WIN_COLS - 1, 0, 2 * WIN_COLS - 2).astype(np.int32)
    mask = jnp.asarray(col_mask)[None, None, :, :, None, :]
    col_bias = rpb[:, :, col_off].astype(jnp.float32)

    qg = q.reshape(b, rows, n_cb, Q_BLOCK_COLS, N_HEADS, HEAD_DIM).transpose(1, 0, 2, 3, 4, 5)
    kg = k.reshape(b, rows, GRID_W, N_HEADS, HEAD_DIM)
    vg = v.reshape(b, rows, GRID_W, N_HEADS, HEAD_DIM)

    def row_block(args):
        q_r, ridx, roff = args
        k_r = kg[:, ridx][:, :, col_idx]
        v_r = vg[:, ridx][:, :, col_idx]
        s = jnp.einsum('bjqhd,brjkhd->bhjqrk', q_r, k_r).astype(jnp.float32)
        s = s + col_bias[:, roff].transpose(0, 2, 3, 1, 4)[None]
        s = jnp.where(mask, s, NEG_INF)
        shp = s.shape
        p = jax.nn.softmax(s.reshape(shp[:4] + (-1,)), axis=-1).reshape(shp).astype(v.dtype)
        return jnp.einsum('bhjqrk,brjkhd->bjqhd', p, v_r)

    o = lax.map(row_block, (qg, jnp.asarray(row_idx), jnp.asarray(row_off)))
    return o.transpose(1, 0, 2, 3, 4, 5).reshape(b, t, D_ATT)


def token_mixer(h, w_in, b_gate, conv_a_w, conv_a_b, ln_a_g, ln_a_b, w_out_a, pool_w, pool_scale,
                w_out_b, sc_w, w_out_c, rpb, w_out_d, w_o):
    b, t, _ = h.shape
    z = h @ w_in
    br_a = conformer_conv(z[..., OFF_A:OFF_B], conv_a_w, conv_a_b, ln_a_g, ln_a_b, w_out_a)
    br_b = multiscale_pool(z[..., OFF_B:OFF_C], pool_w, pool_scale, w_out_b)
    br_c = short_gated_conv(z[..., OFF_C:OFF_D], sc_w, w_out_c)
    q, k, v = jnp.split(z[..., OFF_D:OFF_G], 3, axis=-1)
    q = (q * HEAD_DIM ** -0.5).reshape(b, t, N_HEADS, HEAD_DIM)
    k = k.reshape(b, t, N_HEADS, HEAD_DIM)
    v = v.reshape(b, t, N_HEADS, HEAD_DIM)
    br_d = neighbourhood_attention(q, k, v, rpb) @ w_out_d
    gates = jax.nn.sigmoid(z[..., OFF_G:].reshape(b, t, N_BRANCH, D_MODEL) + b_gate)
    merged = (gates[:, :, 0] * br_a + gates[:, :, 1] * br_b
              + gates[:, :, 2] * br_c + gates[:, :, 3] * br_d)
    return merged @ w_o


def conv_glu_ffn(h, w_up, ffn_conv_w, w_down):
    u = depthwise_conv(h @ w_up, ffn_conv_w)
    val, gate = jnp.split(u, 2, axis=-1)
    return (val * jax.nn.silu(gate)) @ w_down


def trunk(x, norm1_g, w_in, b_gate, conv_a_w, conv_a_b, ln_a_g, ln_a_b, w_out_a, pool_w, pool_scale,
          w_out_b, sc_w, w_out_c, rpb, w_out_d, w_o, norm2_g, w_up, ffn_conv_w, w_down, final_g):
    for l in range(DEPTH):
        h = rms_norm(x, norm1_g[l])
        x = x + token_mixer(h, w_in[l], b_gate[l], conv_a_w[l], conv_a_b[l], ln_a_g[l], ln_a_b[l],
                            w_out_a[l], pool_w[l], pool_scale[l], w_out_b[l], sc_w[l], w_out_c[l],
                            rpb[l], w_out_d[l], w_o[l])
        h = rms_norm(x, norm2_g[l])
        x = x + conv_glu_ffn(h, w_up[l], ffn_conv_w[l], w_down[l])
    return rms_norm(x, final_g)


def setup_inputs(seed: int = 0) -> dict:
    key = jax.random.key(seed)
    ks = jax.random.split(key, 24)
    f32 = jnp.float32

    def nrm(k, shape, s):
        return jax.random.normal(k, shape, f32) * s

    L = DEPTH
    return {
        'x_prompt': nrm(ks[0], (BATCH, SEQ, D_MODEL), 1.0),
        'x_sample': nrm(ks[1], (DEC_BATCH, DEC_SEQ, D_MODEL), 1.0),
        'norm1_g': 1.0 + nrm(ks[2], (L, D_MODEL), 0.05),
        'w_in': nrm(ks[3], (L, D_MODEL, D_IN), D_MODEL ** -0.5),
        'b_gate': nrm(ks[4], (L, N_BRANCH, D_MODEL), 0.1),
        'conv_a_w': nrm(ks[5], (L, CONV_K, D_CONV), CONV_K ** -0.5),
        'conv_a_b': nrm(ks[6], (L, D_CONV), 0.02),
        'ln_a_g': 1.0 + nrm(ks[7], (L, D_CONV), 0.05),
        'ln_a_b': nrm(ks[8], (L, D_CONV), 0.02),
        'w_out_a': nrm(ks[9], (L, D_CONV, D_MODEL), D_CONV ** -0.5),
        'pool_w': nrm(ks[10], (L, N_POOL_GROUPS, POOL_GROUP, POOL_GROUP), POOL_GROUP ** -0.5),
        'pool_scale': 1.0 + nrm(ks[11], (L, D_POOL), 0.1),
        'w_out_b': nrm(ks[12], (L, D_POOL, D_MODEL), D_POOL ** -0.5),
        'sc_w': nrm(ks[13], (L, SC_K, D_SC), SC_K ** -0.5),
        'w_out_c': nrm(ks[14], (L, D_SC, D_MODEL), D_SC ** -0.5),
        'rpb': nrm(ks[15], (L, N_HEADS, 2 * WIN_ROWS - 1, 2 * WIN_COLS - 1), 0.1),
        'w_out_d': nrm(ks[16], (L, D_ATT, D_MODEL), D_ATT ** -0.5),
        'w_o': nrm(ks[17], (L, D_MODEL, D_MODEL), D_MODEL ** -0.5),
        'norm2_g': 1.0 + nrm(ks[18], (L, D_MODEL), 0.05),
        'w_up': nrm(ks[19], (L, D_MODEL, 2 * D_FF), D_MODEL ** -0.5),
        'ffn_conv_w': nrm(ks[20], (L, FFN_K, 2 * D_FF), FFN_K ** -0.5),
        'w_down': nrm(ks[21], (L, D_FF, D_MODEL), D_FF ** -0.5),
        'final_g': 1.0 + nrm(ks[22], (D_MODEL,), 0.05),
    }


def reference(x_prompt, x_sample, norm1_g, w_in, b_gate, conv_a_w, conv_a_b, ln_a_g, ln_a_b, w_out_a,
              pool_w, pool_scale, w_out_b, sc_w, w_out_c, rpb, w_out_d, w_o, norm2_g, w_up, ffn_conv_w,
              w_down, final_g):
    y_prompt = trunk(x_prompt, norm1_g, w_in, b_gate, conv_a_w, conv_a_b, ln_a_g, ln_a_b, w_out_a,
                     pool_w, pool_scale, w_out_b, sc_w, w_out_c, rpb, w_out_d, w_o, norm2_g, w_up,
                     ffn_conv_w, w_down, final_g)
    y_sample = trunk(x_sample, norm1_g, w_in, b_gate, conv_a_w, conv_a_b, ln_a_g, ln_a_b, w_out_a,
                     pool_w, pool_scale, w_out_b, sc_w, w_out_c, rpb, w_out_d, w_o, norm2_g, w_up,
                     ffn_conv_w, w_down, final_g)
    return (y_prompt, y_sample)
```

```python
import functools

import numpy as np
import jax
import jax.numpy as jnp
from jax import lax
from jax.experimental import pallas as pl
from jax.experimental.pallas import tpu as pltpu

F32 = jnp.float32
BF16 = jnp.bfloat16

D_MODEL = 1024
GRID_W = 64
D_CONV = 256
CONV_K = 31
D_POOL = 256
POOL_WINDOWS = (2, 4, 8, 16)
POOL_GROUP = D_POOL // len(POOL_WINDOWS)
D_SC = 256
SC_K = 3
N_HEADS = 4
HEAD_DIM = 64
D_ATT = N_HEADS * HEAD_DIM
WIN_ROWS = 8
WIN_COLS = 16
N_BRANCH = 4
D_FF = 2816
FFN_K = 3
EPS = 1e-6
NEG_INF = -1e30

OFF_B = 2 * D_CONV
OFF_C = OFF_B + D_POOL
OFF_D = OFF_C + 3 * D_SC
OFF_G = OFF_D + 3 * D_ATT
D_GATE = N_BRANCH * D_MODEL
D_ABC = OFF_D

WIN_TOKENS = WIN_ROWS * GRID_W
HALO = 16
FFN_HALO = 8
FF_CHUNK = D_FF // 2

VMEM_LIMIT = 56 * 1024 * 1024
TM = 512


def _const_spec(shape):
    nd = len(shape)
    return pl.BlockSpec(shape, lambda *_: (0,) * nd, pipeline_mode=pl.Buffered(1))


def _rms(x, g):
    ms = jnp.mean(x * x, axis=-1, keepdims=True)
    return x * lax.rsqrt(ms + EPS) * g


def _inproj_kernel(x_ref, g_ref, w_ref, zb_ref, zg_ref):
    h = _rms(x_ref[...], g_ref[...]).astype(BF16)
    zb_ref[...] = jnp.dot(h, w_ref[:, :OFF_G], preferred_element_type=F32).astype(BF16)
    zg_ref[...] = jnp.dot(h, w_ref[:, OFF_G:], preferred_element_type=F32).astype(BF16)


def _inproj(x, g, w):
    n = x.shape[0]
    return pl.pallas_call(
        _inproj_kernel,
        grid=(n // TM,),
        in_specs=[pl.BlockSpec((TM, D_MODEL), lambda i: (i, 0)),
                  _const_spec((1, D_MODEL)),
                  _const_spec(w.shape)],
        out_specs=[pl.BlockSpec((TM, OFF_G), lambda i: (i, 0)),
                   pl.BlockSpec((TM, D_GATE), lambda i: (i, 0))],
        out_shape=[jax.ShapeDtypeStruct((n, OFF_G), BF16),
                   jax.ShapeDtypeStruct((n, D_GATE), BF16)],
        compiler_params=pltpu.CompilerParams(
            dimension_semantics=("parallel",), vmem_limit_bytes=VMEM_LIMIT),
        name="inproj",
    )(x, g, w)


def _attn_kernel(qkv_ref, bias_ref, o_ref, *, rows):
    lane = lax.broadcasted_iota(jnp.int32, (1, D_ATT), 1)
    head_masks = [lane // HEAD_DIM == h for h in range(N_HEADS)]

    def row_body(r, carry):
        rs = jnp.clip(r - WIN_ROWS // 2, 0, rows - WIN_ROWS)
        q0 = pl.multiple_of(r * GRID_W, GRID_W)
        k0 = pl.multiple_of(rs * GRID_W, GRID_W)
        q = qkv_ref[pl.ds(q0, GRID_W), 0:D_ATT]
        k = qkv_ref[pl.ds(k0, WIN_TOKENS), D_ATT:2 * D_ATT]
        v = qkv_ref[pl.ds(k0, WIN_TOKENS), 2 * D_ATT:3 * D_ATT]
        zero = jnp.zeros_like(q)
        qm = jnp.concatenate([jnp.where(m, q, zero) for m in head_masks], axis=0)
        s = lax.dot_general(qm, k, (((1,), (1,)), ((), ())), preferred_element_type=F32)
        s = s + bias_ref[r - rs]
        m = jnp.max(s, axis=-1, keepdims=True)
        p = jnp.exp(s - m)
        l = jnp.sum(p, axis=-1, keepdims=True)
        o = jnp.dot(p.astype(BF16), v, preferred_element_type=F32)
        o = o * (1.0 / l)
        out = jnp.zeros((GRID_W, D_ATT), F32)
        for h, hm in enumerate(head_masks):
            out = jnp.where(hm, o[h * GRID_W:(h + 1) * GRID_W], out)
        o_ref[pl.ds(q0, GRID_W), :] = out.astype(BF16)
        return carry

    lax.fori_loop(0, rows, row_body, 0)


def _attention(zb, bias, t):
    n = zb.shape[0]
    rows = t // GRID_W
    return pl.pallas_call(
        functools.partial(_attn_kernel, rows=rows),
        grid=(n // t,),
        in_specs=[pl.BlockSpec((t, 3 * D_ATT), lambda b: (b, OFF_D // (3 * D_ATT))),
                  _const_spec(bias.shape)],
        out_specs=pl.BlockSpec((t, D_ATT), lambda b: (b, 0)),
        out_shape=jax.ShapeDtypeStruct((n, D_ATT), BF16),
        compiler_params=pltpu.CompilerParams(
            dimension_semantics=("parallel",), vmem_limit_bytes=VMEM_LIMIT),
        name="attention",
    )(zb, bias)


def _attn_bias_table(rpb):
    qcol = np.arange(GRID_W)[:, None]
    kcol = np.arange(GRID_W)[None, :]
    cstart = np.clip(qcol - WIN_COLS // 2, 0, GRID_W - WIN_COLS)
    col_mask = (kcol >= cstart) & (kcol < cstart + WIN_COLS)
    col_off = np.clip(kcol - qcol + WIN_COLS - 1, 0, 2 * WIN_COLS - 2)
    row_off = np.arange(WIN_ROWS)[None, :] - np.arange(WIN_ROWS)[:, None] + WIN_ROWS - 1
    b = rpb.astype(F32)[:, row_off][..., col_off]
    b = jnp.where(jnp.asarray(col_mask), b, NEG_INF)
    return b.transpose(1, 0, 3, 2, 4).reshape(WIN_ROWS, N_HEADS * GRID_W, WIN_TOKENS)


def _mixer_kernel(zc_ref, zp_ref, zn_ref, att_ref, zg_ref, x_ref,
                  caw_ref, cab_ref, lng_ref, lnb_ref, woa_ref,
                  pw_ref, ps_ref, wob_ref, scw_ref, woc_ref, wod_ref,
                  bg_ref, wo_ref, o_ref, a_scr, b_scr, c_scr, *, t):
    tiles_per_seq = t // TM
    i = pl.program_id(0)
    j = i % tiles_per_seq
    is_first = j == 0
    is_last = j == tiles_per_seq - 1
    ext = TM + 2 * HALO

    def stage(z):
        z = z.astype(F32)
        a = z[:, 0:D_CONV] * jax.nn.sigmoid(z[:, D_CONV:OFF_B])
        b = z[:, OFF_B:OFF_C]
        c = z[:, OFF_C + D_SC:OFF_C + 2 * D_SC] * z[:, OFF_C + 2 * D_SC:OFF_D]
        return a, b, c

    keep_p = jnp.where(is_first, 0.0, 1.0)
    keep_n = jnp.where(is_last, 0.0, 1.0)
    ap, bp, cp = stage(zp_ref[...])
    a_scr[0:HALO, :] = ap * keep_p
    b_scr[0:HALO, :] = bp * keep_p
    c_scr[0:HALO, :] = cp * keep_p
    an, bn, cn = stage(zn_ref[...])
    a_scr[HALO + TM:ext, :] = an * keep_n
    b_scr[HALO + TM:ext, :] = bn * keep_n
    c_scr[HALO + TM:ext, :] = cn * keep_n
    ac, bc, cc = stage(zc_ref[...])
    a_scr[HALO:HALO + TM, :] = ac
    b_scr[HALO:HALO + TM, :] = bc
    c_scr[HALO:HALO + TM, :] = cc

    acc = jnp.zeros((TM, D_CONV), F32) + cab_ref[...]
    for k in range(CONV_K):
        acc = acc + caw_ref[k:k + 1, :] * a_scr[HALO - CONV_K // 2 + k:HALO - CONV_K // 2 + k + TM, :]
    mu = jnp.mean(acc, axis=-1, keepdims=True)
    d = acc - mu
    var = jnp.mean(d * d, axis=-1, keepdims=True)
    a = d * lax.rsqrt(var + EPS) * lng_ref[...] + lnb_ref[...]
    a = a * jax.nn.sigmoid(a)
    br_a = jnp.dot(a.astype(BF16), woa_ref[...], preferred_element_type=F32)

    def win(lo, hi):
        return b_scr[HALO + lo:HALO + TM + hi, :]
    s2 = win(-8, 7) + win(-7, 8)
    s4 = s2[0:TM + 13] + s2[2:TM + 15]
    s8 = s4[0:TM + 9] + s4[4:TM + 13]
    s16 = s8[0:TM] + s8[8:TM + 8]
    lane = lax.broadcasted_iota(jnp.int32, (TM, D_POOL), 1)
    grp = lane // POOL_GROUP
    wsum = jnp.where(grp == 0, s2[7:TM + 7],
                     jnp.where(grp == 1, s4[6:TM + 6],
                               jnp.where(grp == 2, s8[4:TM + 4], s16)))
    half = jnp.left_shift(1, grp)
    pos = j * TM + lax.broadcasted_iota(jnp.int32, (TM, D_POOL), 0)
    lo = jnp.maximum(pos - half, 0)
    hi = jnp.minimum(pos + half - 1, t - 1)
    cnt = (hi - lo + 1).astype(F32)
    pool = wsum / cnt - b_scr[HALO:HALO + TM, :]
    pool = jnp.dot(pool.astype(BF16), pw_ref[...], preferred_element_type=F32) * ps_ref[...]
    br_b = jnp.dot(pool.astype(BF16), wob_ref[...], preferred_element_type=F32)

    cacc = jnp.zeros((TM, D_SC), F32)
    for k in range(SC_K):
        cacc = cacc + scw_ref[k:k + 1, :] * c_scr[HALO - SC_K // 2 + k:HALO - SC_K // 2 + k + TM, :]
    cg = zc_ref[:, OFF_C:OFF_C + D_SC].astype(F32) * cacc
    br_c = jnp.dot(cg.astype(BF16), woc_ref[...], preferred_element_type=F32)

    br_d = jnp.dot(att_ref[...], wod_ref[...], preferred_element_type=F32)

    merged = jnp.zeros((TM, D_MODEL), F32)
    for bi, br in enumerate((br_a, br_b, br_c, br_d)):
        g = jax.nn.sigmoid(zg_ref[:, bi * D_MODEL:(bi + 1) * D_MODEL].astype(F32) + bg_ref[bi:bi + 1, :])
        merged = merged + g * br
    o_ref[...] = x_ref[...] + jnp.dot(merged.astype(BF16), wo_ref[...], preferred_element_type=F32)


def _mixer(zb, att, zg, x, wts, t):
    n = x.shape[0]
    hb = TM // HALO
    nhb = n // HALO
    tok = lambda width: pl.BlockSpec((TM, width), lambda i: (i, 0))
    in_specs = [
        tok(D_ABC),
        pl.BlockSpec((HALO, D_ABC), lambda i: (jnp.maximum(i * hb - 1, 0), 0)),
        pl.BlockSpec((HALO, D_ABC), lambda i: (jnp.minimum((i + 1) * hb, nhb - 1), 0)),
        tok(D_ATT), tok(D_GATE), tok(D_MODEL),
    ] + [_const_spec(w.shape) for w in wts]
    ext = TM + 2 * HALO
    return pl.pallas_call(
        functools.partial(_mixer_kernel, t=t),
        grid=(n // TM,),
        in_specs=in_specs,
        out_specs=tok(D_MODEL),
        out_shape=jax.ShapeDtypeStruct((n, D_MODEL), F32),
        scratch_shapes=[pltpu.VMEM((ext, D_CONV), F32),
                        pltpu.VMEM((ext, D_POOL), F32),
                        pltpu.VMEM((ext, D_SC), F32)],
        compiler_params=pltpu.CompilerParams(
            dimension_semantics=("parallel",), vmem_limit_bytes=VMEM_LIMIT),
        name="mixer",
    )(zb, zb, zb, att, zg, x, *wts)


def _ffn_kernel(xc_ref, xp_ref, xn_ref, g_ref, wup_ref, cw_ref, wdn_ref, fg_ref, o_ref,
                h_scr, uv_scr, ug_scr, *, t, final):
    tiles_per_seq = t // TM
    j = pl.program_id(0) % tiles_per_seq
    keep_p = jnp.where(j == 0, 0.0, 1.0)
    keep_n = jnp.where(j == tiles_per_seq - 1, 0.0, 1.0)
    g = g_ref[...]
    x = xc_ref[...]
    h_scr[0:FFN_HALO, :] = (_rms(xp_ref[...], g) * keep_p).astype(BF16)
    h_scr[FFN_HALO:FFN_HALO + TM, :] = _rms(x, g).astype(BF16)
    h_scr[FFN_HALO + TM:FFN_HALO + TM + FFN_HALO, :] = (_rms(xn_ref[...], g) * keep_n).astype(BF16)
    h = h_scr[...]

    acc = x
    for c in range(D_FF // FF_CHUNK):
        c0 = c * FF_CHUNK
        uv_scr[...] = jnp.dot(h, wup_ref[:, c0:c0 + FF_CHUNK], preferred_element_type=F32)
        ug_scr[...] = jnp.dot(h, wup_ref[:, D_FF + c0:D_FF + c0 + FF_CHUNK], preferred_element_type=F32)
        val = jnp.zeros((TM, FF_CHUNK), F32)
        gate = jnp.zeros((TM, FF_CHUNK), F32)
        for k in range(FFN_K):
            r0 = FFN_HALO - FFN_K // 2 + k
            val = val + cw_ref[k:k + 1, c0:c0 + FF_CHUNK] * uv_scr[r0:r0 + TM, :]
            gate = gate + cw_ref[k:k + 1, D_FF + c0:D_FF + c0 + FF_CHUNK] * ug_scr[r0:r0 + TM, :]
        act = val * (gate * jax.nn.sigmoid(gate))
        acc = acc + jnp.dot(act.astype(BF16), wdn_ref[c0:c0 + FF_CHUNK, :], preferred_element_type=F32)
    if final:
        acc = _rms(acc, fg_ref[...])
    o_ref[...] = acc


def _ffn(x, g, wup, cw, wdn, fg, t, final):
    n = x.shape[0]
    hb = TM // FFN_HALO
    nhb = n // FFN_HALO
    ext = TM + 2 * FFN_HALO
    return pl.pallas_call(
        functools.partial(_ffn_kernel, t=t, final=final),
        grid=(n // TM,),
        in_specs=[pl.BlockSpec((TM, D_MODEL), lambda i: (i, 0)),
                  pl.BlockSpec((FFN_HALO, D_MODEL), lambda i: (jnp.maximum(i * hb - 1, 0), 0)),
                  pl.BlockSpec((FFN_HALO, D_MODEL), lambda i: (jnp.minimum((i + 1) * hb, nhb - 1), 0)),
                  _const_spec(g.shape), _const_spec(wup.shape), _const_spec(cw.shape),
                  _const_spec(wdn.shape), _const_spec(fg.shape)],
        out_specs=pl.BlockSpec((TM, D_MODEL), lambda i: (i, 0)),
        out_shape=jax.ShapeDtypeStruct((n, D_MODEL), F32),
        scratch_shapes=[pltpu.VMEM((ext, D_MODEL), BF16),
                        pltpu.VMEM((ext, FF_CHUNK), F32),
                        pltpu.VMEM((ext, FF_CHUNK), F32)],
        compiler_params=pltpu.CompilerParams(
            dimension_semantics=("parallel",), vmem_limit_bytes=VMEM_LIMIT),
        name="ffn",
    )(x, x, x, g, wup, cw, wdn, fg)


def _prep_layer(l, norm1_g, w_in, b_gate, conv_a_w, conv_a_b, ln_a_g, ln_a_b, w_out_a, pool_w,
                pool_scale, w_out_b, sc_w, w_out_c, rpb, w_out_d, w_o, norm2_g, w_up, ffn_conv_w,
                w_down):
    row = lambda v: v.reshape(1, -1).astype(F32)
    col_scale = jnp.ones((w_in.shape[-1],), F32).at[OFF_D:OFF_D + D_ATT].set(HEAD_DIM ** -0.5)
    pool_bd = jax.scipy.linalg.block_diag(*[pool_w[l, gi] for gi in range(len(POOL_WINDOWS))])
    return dict(
        norm1_g=row(norm1_g[l]),
        w_in=(w_in[l] * col_scale).astype(BF16),
        bias=_attn_bias_table(rpb[l]),
        mixer=(conv_a_w[l].astype(F32), row(conv_a_b[l]), row(ln_a_g[l]), row(ln_a_b[l]),
               w_out_a[l].astype(BF16), pool_bd.astype(BF16), row(pool_scale[l]),
               w_out_b[l].astype(BF16), sc_w[l].astype(F32), w_out_c[l].astype(BF16),
               w_out_d[l].astype(BF16), b_gate[l].astype(F32), w_o[l].astype(BF16)),
        norm2_g=row(norm2_g[l]),
        w_up=w_up[l].astype(BF16),
        ffn_conv_w=ffn_conv_w[l].astype(F32),
        w_down=w_down[l].astype(BF16),
    )


def _trunk(x3, layers, final_g):
    b, t, d = x3.shape
    x = x3.reshape(b * t, d)
    fg = final_g.reshape(1, -1).astype(F32)
    for li, p in enumerate(layers):
        zb, zg = _inproj(x, p["norm1_g"], p["w_in"])
        att = _attention(zb, p["bias"], t)
        x = _mixer(zb, att, zg, x, p["mixer"], t)
        x = _ffn(x, p["norm2_g"], p["w_up"], p["ffn_conv_w"], p["w_down"], fg, t,
                 final=li == len(layers) - 1)
    return x.reshape(b, t, d)


def kernel(x_prompt, x_sample, norm1_g, w_in, b_gate, conv_a_w, conv_a_b, ln_a_g, ln_a_b, w_out_a,
           pool_w, pool_scale, w_out_b, sc_w, w_out_c, rpb, w_out_d, w_o, norm2_g, w_up, ffn_conv_w,
           w_down, final_g):
    depth = w_in.shape[0]
    layers = [_prep_layer(l, norm1_g, w_in, b_gate, conv_a_w, conv_a_b, ln_a_g, ln_a_b, w_out_a,
                          pool_w, pool_scale, w_out_b, sc_w, w_out_c, rpb, w_out_d, w_o, norm2_g,
                          w_up, ffn_conv_w, w_down)
              for l in range(depth)]
    return _trunk(x_prompt, layers, final_g), _trunk(x_sample, layers, final_g)
```

```python
import functools

import numpy as np
import jax
import jax.numpy as jnp
from jax import lax
from jax.experimental import pallas as pl
from jax.experimental.pallas import tpu as pltpu

F32 = jnp.float32
BF16 = jnp.bfloat16

D_MODEL = 1024
GRID_W = 64
D_CONV = 256
CONV_K = 31
D_POOL = 256
POOL_WINDOWS = (2, 4, 8, 16)
POOL_GROUP = D_POOL // len(POOL_WINDOWS)
D_SC = 256
SC_K = 3
N_HEADS = 4
HEAD_DIM = 64
D_ATT = N_HEADS * HEAD_DIM
WIN_ROWS = 8
WIN_COLS = 16
N_BRANCH = 4
D_FF = 2816
FFN_K = 3
EPS = 1e-6
NEG_INF = -1e30

OFF_B = 2 * D_CONV
OFF_C = OFF_B + D_POOL
OFF_D = OFF_C + 3 * D_SC
OFF_G = OFF_D + 3 * D_ATT
D_GATE = N_BRANCH * D_MODEL
D_ABC = OFF_D

WIN_TOKENS = WIN_ROWS * GRID_W
LANES = 128
HALO = 16
FFN_HALO = 8
FF_CHUNKS = ((0, 768), (768, 1536), (1536, 2304), (2304, D_FF))

VMEM_LIMIT = 56 * 1024 * 1024
TM = 512


def _const_spec(shape):
    nd = len(shape)
    return pl.BlockSpec(shape, lambda *_: (0,) * nd, pipeline_mode=pl.Buffered(1))


def _rms(x, g):
    ms = jnp.mean(x * x, axis=-1, keepdims=True)
    return x * lax.rsqrt(ms + EPS) * g


def _inproj_kernel(x_ref, g_ref, w_ref, zb_ref, zg_ref):
    h = _rms(x_ref[...], g_ref[...]).astype(BF16)
    zb_ref[...] = jnp.dot(h, w_ref[:, :OFF_G], preferred_element_type=F32).astype(BF16)
    zg_ref[...] = jnp.dot(h, w_ref[:, OFF_G:], preferred_element_type=F32).astype(BF16)


def _inproj(x, g, w):
    n = x.shape[0]
    return pl.pallas_call(
        _inproj_kernel,
        grid=(n // TM,),
        in_specs=[pl.BlockSpec((TM, D_MODEL), lambda i: (i, 0)),
                  _const_spec((1, D_MODEL)),
                  _const_spec(w.shape)],
        out_specs=[pl.BlockSpec((TM, OFF_G), lambda i: (i, 0)),
                   pl.BlockSpec((TM, D_GATE), lambda i: (i, 0))],
        out_shape=[jax.ShapeDtypeStruct((n, OFF_G), BF16),
                   jax.ShapeDtypeStruct((n, D_GATE), BF16)],
        compiler_params=pltpu.CompilerParams(
            dimension_semantics=("parallel",), vmem_limit_bytes=VMEM_LIMIT),
        name="inproj",
    )(x, g, w)


def _attn_kernel(qkv_ref, bias_ref, o_ref, *, rows):
    lane = lax.broadcasted_iota(jnp.int32, (1, D_ATT), 1)
    head_masks = [lane // HEAD_DIM == h for h in range(N_HEADS)]

    def row_body(r, carry):
        rs = jnp.clip(r - WIN_ROWS // 2, 0, rows - WIN_ROWS)
        q0 = pl.multiple_of(r * GRID_W, GRID_W)
        k0 = pl.multiple_of(rs * GRID_W, GRID_W)
        q = qkv_ref[pl.ds(q0, GRID_W), 0:D_ATT]
        k = qkv_ref[pl.ds(k0, WIN_TOKENS), D_ATT:2 * D_ATT]
        v = qkv_ref[pl.ds(k0, WIN_TOKENS), 2 * D_ATT:3 * D_ATT]
        zero = jnp.zeros_like(q)
        qm = jnp.concatenate([jnp.where(m, q, zero) for m in head_masks], axis=0)
        s = lax.dot_general(qm, k, (((1,), (1,)), ((), ())), preferred_element_type=F32)
        s = s + bias_ref[r - rs]
        m = jnp.max(s, axis=-1, keepdims=True)
        p = jnp.exp(s - m)
        l = jnp.sum(p, axis=-1, keepdims=True)
        o = jnp.dot(p.astype(BF16), v, preferred_element_type=F32)
        o = o * (1.0 / l)
        out = jnp.zeros((GRID_W, D_ATT), F32)
        for h, hm in enumerate(head_masks):
            out = jnp.where(hm, o[h * GRID_W:(h + 1) * GRID_W], out)
        o_ref[pl.ds(q0, GRID_W), :] = out.astype(BF16)
        return carry

    lax.fori_loop(0, rows, row_body, 0, unroll=4)


def _attention(zb, bias, t):
    n = zb.shape[0]
    rows = t // GRID_W
    return pl.pallas_call(
        functools.partial(_attn_kernel, rows=rows),
        grid=(n // t,),
        in_specs=[pl.BlockSpec((t, 3 * D_ATT), lambda b: (b, OFF_D // (3 * D_ATT))),
                  _const_spec(bias.shape)],
        out_specs=pl.BlockSpec((t, D_ATT), lambda b: (b, 0)),
        out_shape=jax.ShapeDtypeStruct((n, D_ATT), BF16),
        compiler_params=pltpu.CompilerParams(
            dimension_semantics=("parallel",), vmem_limit_bytes=VMEM_LIMIT),
        name="attention",
    )(zb, bias)


def _attn_bias_table(rpb):
    qcol = np.arange(GRID_W)[:, None]
    kcol = np.arange(GRID_W)[None, :]
    cstart = np.clip(qcol - WIN_COLS // 2, 0, GRID_W - WIN_COLS)
    col_mask = (kcol >= cstart) & (kcol < cstart + WIN_COLS)
    col_off = np.clip(kcol - qcol + WIN_COLS - 1, 0, 2 * WIN_COLS - 2)
    row_off = np.arange(WIN_ROWS)[None, :] - np.arange(WIN_ROWS)[:, None] + WIN_ROWS - 1
    b = rpb.astype(F32)[:, row_off][..., col_off]
    b = jnp.where(jnp.asarray(col_mask), b, NEG_INF)
    return b.transpose(1, 0, 3, 2, 4).reshape(WIN_ROWS, N_HEADS * GRID_W, WIN_TOKENS)


def _mixer_kernel(zc_ref, zp_ref, zn_ref, att_ref, zg_ref, x_ref,
                  caw_ref, cab_ref, lng_ref, lnb_ref, woa_ref,
                  pw_ref, ps_ref, wob_ref, scw_ref, woc_ref, wod_ref,
                  bg_ref, wo_ref, o_ref, a_scr, b_scr, c_scr, p_scr, *, t):
    tiles_per_seq = t // TM
    i = pl.program_id(0)
    j = i % tiles_per_seq
    is_first = j == 0
    is_last = j == tiles_per_seq - 1
    ext = TM + 2 * HALO

    def stage(z):
        z = z.astype(F32)
        a = z[:, 0:D_CONV] * jax.nn.sigmoid(z[:, D_CONV:OFF_B])
        b = z[:, OFF_B:OFF_C]
        c = z[:, OFF_C + D_SC:OFF_C + 2 * D_SC] * z[:, OFF_C + 2 * D_SC:OFF_D]
        return a, b, c

    def put(rows, z, keep):
        for scr, val in zip((a_scr, b_scr, c_scr), stage(z)):
            if keep is not None:
                val = val * keep
            for s in range(val.shape[1] // LANES):
                scr[s, rows, :] = val[:, s * LANES:(s + 1) * LANES]

    put(slice(0, HALO), zp_ref[...], jnp.where(is_first, 0.0, 1.0))
    put(slice(HALO + TM, ext), zn_ref[...], jnp.where(is_last, 0.0, 1.0))
    put(slice(HALO, HALO + TM), zc_ref[...], None)

    def rows(lo, hi):
        return slice(HALO + lo, HALO + TM + hi)

    def depthwise(scr, w_ref, taps):
        cols = []
        for s in range(scr.shape[0]):
            out = None
            for k in range(taps):
                off = k - taps // 2
                term = w_ref[k:k + 1, s * LANES:(s + 1) * LANES] * scr[s, rows(off, off), :]
                out = term if out is None else out + term
            cols.append(out)
        return jnp.concatenate(cols, axis=1)

    acc = depthwise(a_scr, caw_ref, CONV_K) + cab_ref[...]
    mu = jnp.mean(acc, axis=-1, keepdims=True)
    d = acc - mu
    var = jnp.mean(d * d, axis=-1, keepdims=True)
    a = d * lax.rsqrt(var + EPS) * lng_ref[...] + lnb_ref[...]
    a = a * jax.nn.sigmoid(a)
    br_a = jnp.dot(a.astype(BF16), woa_ref[...], preferred_element_type=F32)

    lane = lax.broadcasted_iota(jnp.int32, (TM, LANES), 1)
    low_half = lane < POOL_GROUP
    pos = j * TM + lax.broadcasted_iota(jnp.int32, (TM, LANES), 0)
    pool_cols = []
    for s in range(D_POOL // LANES):
        s2_scr, s4_scr = p_scr.at[0, s], p_scr.at[1, s]
        s8_scr = s2_scr
        s2_scr[rows(-7, 7), :] = b_scr[s, rows(-8, 6), :] + b_scr[s, rows(-7, 7), :]
        s4_scr[rows(-6, 6), :] = s2_scr[rows(-7, 5), :] + s2_scr[rows(-5, 7), :]
        if s == 0:
            wsum = jnp.where(low_half, s2_scr[rows(0, 0), :], s4_scr[rows(0, 0), :])
        else:
            s8_scr[rows(-4, 4), :] = s4_scr[rows(-6, 2), :] + s4_scr[rows(-2, 6), :]
            s16 = s8_scr[rows(-4, -4), :] + s8_scr[rows(4, 4), :]
            wsum = jnp.where(low_half, s8_scr[rows(0, 0), :], s16)
        half = jnp.where(low_half, POOL_WINDOWS[2 * s] // 2, POOL_WINDOWS[2 * s + 1] // 2)
        cnt = (jnp.minimum(pos + half - 1, t - 1) - jnp.maximum(pos - half, 0) + 1).astype(F32)
        pool_cols.append(wsum / cnt - b_scr[s, rows(0, 0), :])
    pool = jnp.concatenate(pool_cols, axis=1)
    pool = jnp.dot(pool.astype(BF16), pw_ref[...], preferred_element_type=F32) * ps_ref[...]
    br_b = jnp.dot(pool.astype(BF16), wob_ref[...], preferred_element_type=F32)

    cg = zc_ref[:, OFF_C:OFF_C + D_SC].astype(F32) * depthwise(c_scr, scw_ref, SC_K)
    br_c = jnp.dot(cg.astype(BF16), woc_ref[...], preferred_element_type=F32)

    br_d = jnp.dot(att_ref[...], wod_ref[...], preferred_element_type=F32)

    merged = jnp.zeros((TM, D_MODEL), F32)
    for bi, br in enumerate((br_a, br_b, br_c, br_d)):
        g = jax.nn.sigmoid(zg_ref[:, bi * D_MODEL:(bi + 1) * D_MODEL].astype(F32) + bg_ref[bi:bi + 1, :])
        merged = merged + g * br
    o_ref[...] = x_ref[...] + jnp.dot(merged.astype(BF16), wo_ref[...], preferred_element_type=F32)


def _mixer(zb, att, zg, x, wts, t):
    n = x.shape[0]
    hb = TM // HALO
    nhb = n // HALO
    tok = lambda width: pl.BlockSpec((TM, width), lambda i: (i, 0))
    in_specs = [
        tok(D_ABC),
        pl.BlockSpec((HALO, D_ABC), lambda i: (jnp.maximum(i * hb - 1, 0), 0)),
        pl.BlockSpec((HALO, D_ABC), lambda i: (jnp.minimum((i + 1) * hb, nhb - 1), 0)),
        tok(D_ATT), tok(D_GATE), tok(D_MODEL),
    ] + [_const_spec(w.shape) for w in wts]
    ext = TM + 2 * HALO
    return pl.pallas_call(
        functools.partial(_mixer_kernel, t=t),
        grid=(n // TM,),
        in_specs=in_specs,
        out_specs=tok(D_MODEL),
        out_shape=jax.ShapeDtypeStruct((n, D_MODEL), F32),
        scratch_shapes=[pltpu.VMEM((D_CONV // LANES, ext, LANES), F32),
                        pltpu.VMEM((D_POOL // LANES, ext, LANES), F32),
                        pltpu.VMEM((D_SC // LANES, ext, LANES), F32),
                        pltpu.VMEM((2, D_POOL // LANES, ext, LANES), F32)],
        compiler_params=pltpu.CompilerParams(
            dimension_semantics=("parallel",), vmem_limit_bytes=VMEM_LIMIT),
        name="mixer",
    )(zb, zb, zb, att, zg, x, *wts)


def _ffn_kernel(xc_ref, xp_ref, xn_ref, g_ref, wup_ref, cw_ref, wdn_ref, fg_ref, o_ref,
                h_scr, u_scr, *, t, final):
    tiles_per_seq = t // TM
    j = pl.program_id(0) % tiles_per_seq
    keep_p = jnp.where(j == 0, 0.0, 1.0)
    keep_n = jnp.where(j == tiles_per_seq - 1, 0.0, 1.0)
    g = g_ref[...]
    x = xc_ref[...]
    h_scr[0:FFN_HALO, :] = (_rms(xp_ref[...], g) * keep_p).astype(BF16)
    h_scr[FFN_HALO:FFN_HALO + TM, :] = _rms(x, g).astype(BF16)
    h_scr[FFN_HALO + TM:FFN_HALO + TM + FFN_HALO, :] = (_rms(xn_ref[...], g) * keep_n).astype(BF16)
    h = h_scr[...]

    def up(c0, c1):
        for off in (0, D_FF):
            u = jnp.dot(h, wup_ref[:, off + c0:off + c1], preferred_element_type=F32)
            for s in range((c1 - c0) // LANES):
                u_scr[(off + c0) // LANES + s] = u[:, s * LANES:(s + 1) * LANES]

    def conv(slab):
        out = None
        for k in range(FFN_K):
            r0 = FFN_HALO - FFN_K // 2 + k
            term = cw_ref[k:k + 1, slab * LANES:(slab + 1) * LANES] * u_scr[slab, r0:r0 + TM, :]
            out = term if out is None else out + term
        return out

    def glu(c0, c1):
        cols = []
        for s in range(c0 // LANES, c1 // LANES):
            val = conv(s)
            gate = conv(D_FF // LANES + s)
            cols.append((val * (gate * jax.nn.sigmoid(gate))).astype(BF16))
        return jnp.concatenate(cols, axis=1)

    acc = x
    up(*FF_CHUNKS[0])
    for ci, (c0, c1) in enumerate(FF_CHUNKS):
        if ci + 1 < len(FF_CHUNKS):
            up(*FF_CHUNKS[ci + 1])
        acc = acc + jnp.dot(glu(c0, c1), wdn_ref[c0:c1, :], preferred_element_type=F32)
    if final:
        acc = _rms(acc, fg_ref[...])
    o_ref[...] = acc


def _ffn(x, g, wup, cw, wdn, fg, t, final):
    n = x.shape[0]
    hb = TM // FFN_HALO
    nhb = n // FFN_HALO
    ext = TM + 2 * FFN_HALO
    return pl.pallas_call(
        functools.partial(_ffn_kernel, t=t, final=final),
        grid=(n // TM,),
        in_specs=[pl.BlockSpec((TM, D_MODEL), lambda i: (i, 0)),
                  pl.BlockSpec((FFN_HALO, D_MODEL), lambda i: (jnp.maximum(i * hb - 1, 0), 0)),
                  pl.BlockSpec((FFN_HALO, D_MODEL), lambda i: (jnp.minimum((i + 1) * hb, nhb - 1), 0)),
                  _const_spec(g.shape), _const_spec(wup.shape), _const_spec(cw.shape),
                  _const_spec(wdn.shape), _const_spec(fg.shape)],
        out_specs=pl.BlockSpec((TM, D_MODEL), lambda i: (i, 0)),
        out_shape=jax.ShapeDtypeStruct((n, D_MODEL), F32),
        scratch_shapes=[pltpu.VMEM((ext, D_MODEL), BF16),
                        pltpu.VMEM((2 * D_FF // LANES, ext, LANES), F32)],
        compiler_params=pltpu.CompilerParams(
            dimension_semantics=("parallel",), vmem_limit_bytes=VMEM_LIMIT),
        name="ffn",
    )(x, x, x, g, wup, cw, wdn, fg)


def _prep_layer(l, norm1_g, w_in, b_gate, conv_a_w, conv_a_b, ln_a_g, ln_a_b, w_out_a, pool_w,
                pool_scale, w_out_b, sc_w, w_out_c, rpb, w_out_d, w_o, norm2_g, w_up, ffn_conv_w,
                w_down):
    row = lambda v: v.reshape(1, -1).astype(F32)
    col_scale = jnp.ones((w_in.shape[-1],), F32).at[OFF_D:OFF_D + D_ATT].set(HEAD_DIM ** -0.5)
    pool_bd = jax.scipy.linalg.block_diag(*[pool_w[l, gi] for gi in range(len(POOL_WINDOWS))])
    return dict(
        norm1_g=row(norm1_g[l]),
        w_in=(w_in[l] * col_scale).astype(BF16),
        bias=_attn_bias_table(rpb[l]),
        mixer=(conv_a_w[l].astype(F32), row(conv_a_b[l]), row(ln_a_g[l]), row(ln_a_b[l]),
               w_out_a[l].astype(BF16), pool_bd.astype(BF16), row(pool_scale[l]),
               w_out_b[l].astype(BF16), sc_w[l].astype(F32), w_out_c[l].astype(BF16),
               w_out_d[l].astype(BF16), b_gate[l].astype(F32), w_o[l].astype(BF16)),
        norm2_g=row(norm2_g[l]),
        w_up=w_up[l].astype(BF16),
        ffn_conv_w=ffn_conv_w[l].astype(F32),
        w_down=w_down[l].astype(BF16),
    )


def _trunk(x3, layers, final_g):
    b, t, d = x3.shape
    x = x3.reshape(b * t, d)
    fg = final_g.reshape(1, -1).astype(F32)
    for li, p in enumerate(layers):
        zb, zg = _inproj(x, p["norm1_g"], p["w_in"])
        att = _attention(zb, p["bias"], t)
        x = _mixer(zb, att, zg, x, p["mixer"], t)
        x = _ffn(x, p["norm2_g"], p["w_up"], p["ffn_conv_w"], p["w_down"], fg, t,
                 final=li == len(layers) - 1)
    return x.reshape(b, t, d)


def kernel(x_prompt, x_sample, norm1_g, w_in, b_gate, conv_a_w, conv_a_b, ln_a_g, ln_a_b, w_out_a,
           pool_w, pool_scale, w_out_b, sc_w, w_out_c, rpb, w_out_d, w_o, norm2_g, w_up, ffn_conv_w,
           w_down, final_g):
    depth = w_in.shape[0]
    layers = [_prep_layer(l, norm1_g, w_in, b_gate, conv_a_w, conv_a_b, ln_a_g, ln_a_b, w_out_a,
                          pool_w, pool_scale, w_out_b, sc_w, w_out_c, rpb, w_out_d, w_o, norm2_g,
                          w_up, ffn_conv_w, w_down)
              for l in range(depth)]
    return _trunk(x_prompt, layers, final_g), _trunk(x_sample, layers, final_g)
```

```python
import functools

import numpy as np
import jax
import jax.numpy as jnp
from jax import lax
from jax.experimental import pallas as pl
from jax.experimental.pallas import tpu as pltpu

F32 = jnp.float32
BF16 = jnp.bfloat16

D_MODEL = 1024
GRID_W = 64
D_CONV = 256
CONV_K = 31
D_POOL = 256
POOL_WINDOWS = (2, 4, 8, 16)
POOL_GROUP = D_POOL // len(POOL_WINDOWS)
D_SC = 256
SC_K = 3
N_HEADS = 4
HEAD_DIM = 64
D_ATT = N_HEADS * HEAD_DIM
WIN_ROWS = 8
WIN_COLS = 16
N_BRANCH = 4
D_FF = 2816
FFN_K = 3
EPS = 1e-6
NEG_INF = -1e30

OFF_B = 2 * D_CONV
OFF_C = OFF_B + D_POOL
OFF_D = OFF_C + 3 * D_SC
OFF_G = OFF_D + 3 * D_ATT
D_GATE = N_BRANCH * D_MODEL
D_MIX = D_CONV + D_POOL + 2 * D_SC

WIN_TOKENS = WIN_ROWS * GRID_W
LANES = 128
HALO = 16
FFN_HALO = 8
FF_CHUNKS = ((0, 768), (768, 1536), (1536, 2304), (2304, D_FF))

VMEM_LIMIT = 56 * 1024 * 1024
TM = 512


def _const_spec(shape):
    nd = len(shape)
    return pl.BlockSpec(shape, lambda *_: (0,) * nd, pipeline_mode=pl.Buffered(1))


def _rms(x, g):
    ms = jnp.mean(x * x, axis=-1, keepdims=True)
    return x * lax.rsqrt(ms + EPS) * g


def _half_sigmoid(half_x):
    return 0.5 * jnp.tanh(half_x) + 0.5


def _inproj_kernel(x_ref, g_ref, w_ref, hbg_ref, zm_ref, zq_ref, gate_ref):
    h = _rms(x_ref[...], g_ref[...]).astype(BF16)
    z = jnp.dot(h, w_ref[:, :OFF_D], preferred_element_type=F32)
    a = z[:, 0:D_CONV] * jax.nn.sigmoid(z[:, D_CONV:OFF_B])
    cx = z[:, OFF_C + D_SC:OFF_C + 2 * D_SC] * z[:, OFF_C + 2 * D_SC:OFF_D]
    zm_ref[...] = jnp.concatenate([a, z[:, OFF_B:OFF_C + D_SC], cx], axis=1).astype(BF16)
    zq_ref[...] = jnp.dot(h, w_ref[:, OFF_D:OFF_G], preferred_element_type=F32).astype(BF16)
    for bi in range(N_BRANCH):
        cols = slice(bi * D_MODEL, (bi + 1) * D_MODEL)
        zg = jnp.dot(h, w_ref[:, OFF_G + bi * D_MODEL:OFF_G + (bi + 1) * D_MODEL],
                     preferred_element_type=F32)
        gate_ref[:, cols] = _half_sigmoid(zg + hbg_ref[:, cols]).astype(BF16)


def _inproj(x, g, w, hbg):
    n = x.shape[0]
    return pl.pallas_call(
        _inproj_kernel,
        grid=(n // TM,),
        in_specs=[pl.BlockSpec((TM, D_MODEL), lambda i: (i, 0)),
                  _const_spec((1, D_MODEL)),
                  _const_spec(w.shape),
                  _const_spec(hbg.shape)],
        out_specs=[pl.BlockSpec((TM, D_MIX), lambda i: (i, 0)),
                   pl.BlockSpec((TM, 3 * D_ATT), lambda i: (i, 0)),
                   pl.BlockSpec((TM, D_GATE), lambda i: (i, 0))],
        out_shape=[jax.ShapeDtypeStruct((n, D_MIX), BF16),
                   jax.ShapeDtypeStruct((n, 3 * D_ATT), BF16),
                   jax.ShapeDtypeStruct((n, D_GATE), BF16)],
        compiler_params=pltpu.CompilerParams(
            dimension_semantics=("parallel",), vmem_limit_bytes=VMEM_LIMIT),
        name="inproj",
    )(x, g, w, hbg)


def _attn_kernel(qkv_ref, bias_ref, o_ref, *, rows):
    lane = lax.broadcasted_iota(jnp.int32, (1, D_ATT), 1)
    head_masks = [lane // HEAD_DIM == h for h in range(N_HEADS)]

    def row_body(r, carry):
        rs = jnp.clip(r - WIN_ROWS // 2, 0, rows - WIN_ROWS)
        q0 = pl.multiple_of(r * GRID_W, GRID_W)
        k0 = pl.multiple_of(rs * GRID_W, GRID_W)
        q = qkv_ref[pl.ds(q0, GRID_W), 0:D_ATT]
        k = qkv_ref[pl.ds(k0, WIN_TOKENS), D_ATT:2 * D_ATT]
        v = qkv_ref[pl.ds(k0, WIN_TOKENS), 2 * D_ATT:3 * D_ATT]
        zero = jnp.zeros_like(q)
        qm = jnp.concatenate([jnp.where(m, q, zero) for m in head_masks], axis=0)
        s = lax.dot_general(qm, k, (((1,), (1,)), ((), ())), preferred_element_type=F32)
        s = s + bias_ref[r - rs]
        m = jnp.max(s, axis=-1, keepdims=True)
        p = jnp.exp(s - m)
        l = jnp.sum(p, axis=-1, keepdims=True)
        o = jnp.dot(p.astype(BF16), v, preferred_element_type=F32)
        o = o * (1.0 / l)
        out = jnp.zeros((GRID_W, D_ATT), F32)
        for h, hm in enumerate(head_masks):
            out = jnp.where(hm, o[h * GRID_W:(h + 1) * GRID_W], out)
        o_ref[pl.ds(q0, GRID_W), :] = out.astype(BF16)
        return carry

    lax.fori_loop(0, rows, row_body, 0, unroll=8)


def _attention(zq, bias, t):
    n = zq.shape[0]
    rows = t // GRID_W
    return pl.pallas_call(
        functools.partial(_attn_kernel, rows=rows),
        grid=(n // t,),
        in_specs=[pl.BlockSpec((t, 3 * D_ATT), lambda b: (b, 0)),
                  _const_spec(bias.shape)],
        out_specs=pl.BlockSpec((t, D_ATT), lambda b: (b, 0)),
        out_shape=jax.ShapeDtypeStruct((n, D_ATT), BF16),
        compiler_params=pltpu.CompilerParams(
            dimension_semantics=("parallel",), vmem_limit_bytes=VMEM_LIMIT),
        name="attention",
    )(zq, bias)


def _attn_bias_table(rpb):
    qcol = np.arange(GRID_W)[:, None]
    kcol = np.arange(GRID_W)[None, :]
    cstart = np.clip(qcol - WIN_COLS // 2, 0, GRID_W - WIN_COLS)
    col_mask = (kcol >= cstart) & (kcol < cstart + WIN_COLS)
    col_off = np.clip(kcol - qcol + WIN_COLS - 1, 0, 2 * WIN_COLS - 2)
    row_off = np.arange(WIN_ROWS)[None, :] - np.arange(WIN_ROWS)[:, None] + WIN_ROWS - 1
    b = rpb.astype(F32)[:, row_off][..., col_off]
    b = jnp.where(jnp.asarray(col_mask), b, NEG_INF)
    return b.transpose(1, 0, 3, 2, 4).reshape(WIN_ROWS, N_HEADS * GRID_W, WIN_TOKENS)


def _mixer_kernel(zc_ref, zp_ref, zn_ref, att_ref, gate_ref, x_ref,
                  caw_ref, cab_ref, lng_ref, lnb_ref, woa_ref,
                  pw_ref, ps_ref, wob_ref, scw_ref, woc_ref, wod_ref,
                  wo_ref, o_ref, a_scr, b_scr, c_scr, p_scr, *, t):
    tiles_per_seq = t // TM
    i = pl.program_id(0)
    j = i % tiles_per_seq
    is_first = j == 0
    is_last = j == tiles_per_seq - 1
    ext = TM + 2 * HALO
    off_gb = D_CONV + D_POOL

    def stage(z):
        return (z[:, 0:D_CONV].astype(F32), z[:, D_CONV:off_gb].astype(F32),
                z[:, off_gb + D_SC:D_MIX].astype(F32))

    def put(rows, z, keep):
        for scr, val in zip((a_scr, b_scr, c_scr), stage(z)):
            if keep is not None:
                val = val * keep
            for s in range(val.shape[1] // LANES):
                scr[s, rows, :] = val[:, s * LANES:(s + 1) * LANES]

    put(slice(0, HALO), zp_ref[...], jnp.where(is_first, 0.0, 1.0))
    put(slice(HALO + TM, ext), zn_ref[...], jnp.where(is_last, 0.0, 1.0))
    put(slice(HALO, HALO + TM), zc_ref[...], None)

    def rows(lo, hi):
        return slice(HALO + lo, HALO + TM + hi)

    def depthwise(scr, w_ref, taps):
        cols = []
        for s in range(scr.shape[0]):
            out = None
            for k in range(taps):
                off = k - taps // 2
                term = w_ref[k:k + 1, s * LANES:(s + 1) * LANES] * scr[s, rows(off, off), :]
                out = term if out is None else out + term
            cols.append(out)
        return jnp.concatenate(cols, axis=1)

    acc = depthwise(a_scr, caw_ref, CONV_K) + cab_ref[...]
    mu = jnp.mean(acc, axis=-1, keepdims=True)
    d = acc - mu
    var = jnp.mean(d * d, axis=-1, keepdims=True)
    a = d * lax.rsqrt(var + EPS) * lng_ref[...] + lnb_ref[...]
    a = a * jax.nn.sigmoid(a)
    br_a = jnp.dot(a.astype(BF16), woa_ref[...], preferred_element_type=F32)

    lane = lax.broadcasted_iota(jnp.int32, (TM, LANES), 1)
    low_half = lane < POOL_GROUP
    pos = j * TM + lax.broadcasted_iota(jnp.int32, (TM, LANES), 0)
    pool_cols = []
    for s in range(D_POOL // LANES):
        s2_scr, s4_scr = p_scr.at[0, s], p_scr.at[1, s]
        s8_scr = s2_scr
        s2_scr[rows(-7, 7), :] = b_scr[s, rows(-8, 6), :] + b_scr[s, rows(-7, 7), :]
        s4_scr[rows(-6, 6), :] = s2_scr[rows(-7, 5), :] + s2_scr[rows(-5, 7), :]
        if s == 0:
            wsum = jnp.where(low_half, s2_scr[rows(0, 0), :], s4_scr[rows(0, 0), :])
        else:
            s8_scr[rows(-4, 4), :] = s4_scr[rows(-6, 2), :] + s4_scr[rows(-2, 6), :]
            s16 = s8_scr[rows(-4, -4), :] + s8_scr[rows(4, 4), :]
            wsum = jnp.where(low_half, s8_scr[rows(0, 0), :], s16)
        half = jnp.where(low_half, POOL_WINDOWS[2 * s] // 2, POOL_WINDOWS[2 * s + 1] // 2)
        cnt = (jnp.minimum(pos + half - 1, t - 1) - jnp.maximum(pos - half, 0) + 1).astype(F32)
        pool_cols.append(wsum / cnt - b_scr[s, rows(0, 0), :])
    pool = jnp.concatenate(pool_cols, axis=1)
    pool = jnp.dot(pool.astype(BF16), pw_ref[...], preferred_element_type=F32) * ps_ref[...]
    br_b = jnp.dot(pool.astype(BF16), wob_ref[...], preferred_element_type=F32)

    cg = zc_ref[:, off_gb:off_gb + D_SC].astype(F32) * depthwise(c_scr, scw_ref, SC_K)
    br_c = jnp.dot(cg.astype(BF16), woc_ref[...], preferred_element_type=F32)

    br_d = jnp.dot(att_ref[...], wod_ref[...], preferred_element_type=F32)

    merged = None
    for bi, br in enumerate((br_a, br_b, br_c, br_d)):
        term = gate_ref[:, bi * D_MODEL:(bi + 1) * D_MODEL].astype(F32) * br
        merged = term if merged is None else merged + term
    o_ref[...] = x_ref[...] + jnp.dot(merged.astype(BF16), wo_ref[...], preferred_element_type=F32)


def _mixer(zm, att, gate, x, wts, t):
    n = x.shape[0]
    hb = TM // HALO
    nhb = n // HALO
    tok = lambda width: pl.BlockSpec((TM, width), lambda i: (i, 0))
    in_specs = [
        tok(D_MIX),
        pl.BlockSpec((HALO, D_MIX), lambda i: (jnp.maximum(i * hb - 1, 0), 0)),
        pl.BlockSpec((HALO, D_MIX), lambda i: (jnp.minimum((i + 1) * hb, nhb - 1), 0)),
        tok(D_ATT), tok(D_GATE), tok(D_MODEL),
    ] + [_const_spec(w.shape) for w in wts]
    ext = TM + 2 * HALO
    return pl.pallas_call(
        functools.partial(_mixer_kernel, t=t),
        grid=(n // TM,),
        in_specs=in_specs,
        out_specs=tok(D_MODEL),
        out_shape=jax.ShapeDtypeStruct((n, D_MODEL), F32),
        scratch_shapes=[pltpu.VMEM((D_CONV // LANES, ext, LANES), F32),
                        pltpu.VMEM((D_POOL // LANES, ext, LANES), F32),
                        pltpu.VMEM((D_SC // LANES, ext, LANES), F32),
                        pltpu.VMEM((2, D_POOL // LANES, ext, LANES), F32)],
        compiler_params=pltpu.CompilerParams(
            dimension_semantics=("parallel",), vmem_limit_bytes=VMEM_LIMIT),
        name="mixer",
    )(zm, zm, zm, att, gate, x, *wts)


def _ffn_kernel(xc_ref, xp_ref, xn_ref, g_ref, wup_ref, cw_ref, wdn_ref, fg_ref, o_ref,
                h_scr, u_scr, *, t, final):
    tiles_per_seq = t // TM
    j = pl.program_id(0) % tiles_per_seq
    keep_p = jnp.where(j == 0, 0.0, 1.0)
    keep_n = jnp.where(j == tiles_per_seq - 1, 0.0, 1.0)
    g = g_ref[...]
    x = xc_ref[...]
    h_scr[0:FFN_HALO, :] = (_rms(xp_ref[...], g) * keep_p).astype(BF16)
    h_scr[FFN_HALO:FFN_HALO + TM, :] = _rms(x, g).astype(BF16)
    h_scr[FFN_HALO + TM:FFN_HALO + TM + FFN_HALO, :] = (_rms(xn_ref[...], g) * keep_n).astype(BF16)
    h = h_scr[...]

    def up(c0, c1):
        for off in (0, D_FF):
            u = jnp.dot(h, wup_ref[:, off + c0:off + c1], preferred_element_type=F32)
            for s in range((c1 - c0) // LANES):
                u_scr[(off + c0) // LANES + s] = u[:, s * LANES:(s + 1) * LANES]

    def conv(slab):
        out = None
        for k in range(FFN_K):
            r0 = FFN_HALO - FFN_K // 2 + k
            term = cw_ref[k:k + 1, slab * LANES:(slab + 1) * LANES] * u_scr[slab, r0:r0 + TM, :]
            out = term if out is None else out + term
        return out

    def glu(c0, c1):
        cols = []
        for s in range(c0 // LANES, c1 // LANES):
            val = conv(s)
            hg = conv(D_FF // LANES + s)
            cols.append((val * (hg * (1.0 + jnp.tanh(hg)))).astype(BF16))
        return jnp.concatenate(cols, axis=1)

    acc = x
    up(*FF_CHUNKS[0])
    for ci, (c0, c1) in enumerate(FF_CHUNKS):
        if ci + 1 < len(FF_CHUNKS):
            up(*FF_CHUNKS[ci + 1])
        acc = acc + jnp.dot(glu(c0, c1), wdn_ref[c0:c1, :], preferred_element_type=F32)
    if final:
        acc = _rms(acc, fg_ref[...])
    o_ref[...] = acc


def _ffn(x, g, wup, cw, wdn, fg, t, final):
    n = x.shape[0]
    hb = TM // FFN_HALO
    nhb = n // FFN_HALO
    ext = TM + 2 * FFN_HALO
    return pl.pallas_call(
        functools.partial(_ffn_kernel, t=t, final=final),
        grid=(n // TM,),
        in_specs=[pl.BlockSpec((TM, D_MODEL), lambda i: (i, 0)),
                  pl.BlockSpec((FFN_HALO, D_MODEL), lambda i: (jnp.maximum(i * hb - 1, 0), 0)),
                  pl.BlockSpec((FFN_HALO, D_MODEL), lambda i: (jnp.minimum((i + 1) * hb, nhb - 1), 0)),
                  _const_spec(g.shape), _const_spec(wup.shape), _const_spec(cw.shape),
                  _const_spec(wdn.shape), _const_spec(fg.shape)],
        out_specs=pl.BlockSpec((TM, D_MODEL), lambda i: (i, 0)),
        out_shape=jax.ShapeDtypeStruct((n, D_MODEL), F32),
        scratch_shapes=[pltpu.VMEM((ext, D_MODEL), BF16),
                        pltpu.VMEM((2 * D_FF // LANES, ext, LANES), F32)],
        compiler_params=pltpu.CompilerParams(
            dimension_semantics=("parallel",), vmem_limit_bytes=VMEM_LIMIT),
        name="ffn",
    )(x, x, x, g, wup, cw, wdn, fg)


def _prep_layer(l, norm1_g, w_in, b_gate, conv_a_w, conv_a_b, ln_a_g, ln_a_b, w_out_a, pool_w,
                pool_scale, w_out_b, sc_w, w_out_c, rpb, w_out_d, w_o, norm2_g, w_up, ffn_conv_w,
                w_down):
    row = lambda v: v.reshape(1, -1).astype(F32)
    col_scale = (jnp.ones((w_in.shape[-1],), F32).at[OFF_D:OFF_D + D_ATT].set(HEAD_DIM ** -0.5)
                 .at[OFF_G:].set(0.5))
    up_scale = jnp.ones((w_up.shape[-1],), F32).at[D_FF:].set(0.5)
    pool_bd = jax.scipy.linalg.block_diag(*[pool_w[l, gi] for gi in range(len(POOL_WINDOWS))])
    return dict(
        norm1_g=row(norm1_g[l]),
        w_in=(w_in[l] * col_scale).astype(BF16),
        half_b_gate=row(b_gate[l]) * 0.5,
        bias=_attn_bias_table(rpb[l]),
        mixer=(conv_a_w[l].astype(F32), row(conv_a_b[l]), row(ln_a_g[l]), row(ln_a_b[l]),
               w_out_a[l].astype(BF16), pool_bd.astype(BF16), row(pool_scale[l]),
               w_out_b[l].astype(BF16), sc_w[l].astype(F32), w_out_c[l].astype(BF16),
               w_out_d[l].astype(BF16), w_o[l].astype(BF16)),
        norm2_g=row(norm2_g[l]),
        w_up=(w_up[l] * up_scale).astype(BF16),
        ffn_conv_w=ffn_conv_w[l].astype(F32),
        w_down=w_down[l].astype(BF16),
    )


def _trunk(x3, layers, final_g):
    b, t, d = x3.shape
    x = x3.reshape(b * t, d)
    fg = final_g.reshape(1, -1).astype(F32)
    for li, p in enumerate(layers):
        zm, zq, gate = _inproj(x, p["norm1_g"], p["w_in"], p["half_b_gate"])
        att = _attention(zq, p["bias"], t)
        x = _mixer(zm, att, gate, x, p["mixer"], t)
        x = _ffn(x, p["norm2_g"], p["w_up"], p["ffn_conv_w"], p["w_down"], fg, t,
                 final=li == len(layers) - 1)
    return x.reshape(b, t, d)


def kernel(x_prompt, x_sample, norm1_g, w_in, b_gate, conv_a_w, conv_a_b, ln_a_g, ln_a_b, w_out_a,
           pool_w, pool_scale, w_out_b, sc_w, w_out_c, rpb, w_out_d, w_o, norm2_g, w_up, ffn_conv_w,
           w_down, final_g):
    depth = w_in.shape[0]
    layers = [_prep_layer(l, norm1_g, w_in, b_gate, conv_a_w, conv_a_b, ln_a_g, ln_a_b, w_out_a,
                          pool_w, pool_scale, w_out_b, sc_w, w_out_c, rpb, w_out_d, w_o, norm2_g,
                          w_up, ffn_conv_w, w_down)
              for l in range(depth)]
    return _trunk(x_prompt, layers, final_g), _trunk(x_sample, layers, final_g)
```

```python
import functools

import numpy as np
import jax
import jax.numpy as jnp
from jax import lax
from jax.experimental import pallas as pl
from jax.experimental.pallas import tpu as pltpu

F32 = jnp.float32
BF16 = jnp.bfloat16

D_MODEL = 1024
GRID_W = 64
D_CONV = 256
CONV_K = 31
D_POOL = 256
POOL_WINDOWS = (2, 4, 8, 16)
POOL_GROUP = D_POOL // len(POOL_WINDOWS)
D_SC = 256
SC_K = 3
N_HEADS = 4
HEAD_DIM = 64
D_ATT = N_HEADS * HEAD_DIM
WIN_ROWS = 8
WIN_COLS = 16
N_BRANCH = 4
D_FF = 2816
FFN_K = 3
EPS = 1e-6
NEG_INF = -1e30

OFF_B = 2 * D_CONV
OFF_C = OFF_B + D_POOL
OFF_D = OFF_C + 3 * D_SC
OFF_G = OFF_D + 3 * D_ATT
D_GATE = N_BRANCH * D_MODEL
D_ACT = D_CONV + D_POOL + D_SC

WIN_TOKENS = WIN_ROWS * GRID_W
LANES = 128
HALO = 16
GATE_CHUNKS = 16
PACE_SPREAD = 2
FFN_HALO = 8
FF_CHUNKS = ((0, 768), (768, 1536), (1536, 2304), (2304, D_FF))

VMEM_LIMIT = 56 * 1024 * 1024
TM = 512


def _const_spec(shape):
    nd = len(shape)
    return pl.BlockSpec(shape, lambda *_: (0,) * nd, pipeline_mode=pl.Buffered(1))


def _rms(x, g):
    ms = jnp.mean(x * x, axis=-1, keepdims=True)
    return x * lax.rsqrt(ms + EPS) * g


def _half_sigmoid(half_x):
    return 0.5 * jnp.tanh(half_x) + 0.5


def _inproj_kernel(xc_ref, xp_ref, xn_ref, g_ref, w_ref, hbg_ref,
                   caw_ref, cab_ref, lng_ref, lnb_ref, scw_ref,
                   act_ref, zq_ref, gate_ref, h_scr, a_scr, b_scr, c_scr, p_scr, *, t):
    tiles_per_seq = t // TM
    j = pl.program_id(0) % tiles_per_seq
    keep_p = jnp.where(j == 0, 0.0, 1.0)
    keep_n = jnp.where(j == tiles_per_seq - 1, 0.0, 1.0)
    ext = TM + 2 * HALO
    g = g_ref[...]
    h_scr[0:HALO, :] = _rms(xp_ref[...], g).astype(BF16)
    h_scr[HALO:HALO + TM, :] = _rms(xc_ref[...], g).astype(BF16)
    h_scr[HALO + TM:ext, :] = _rms(xn_ref[...], g).astype(BF16)

    h_ext = h_scr[...]
    za = jnp.dot(h_ext, w_ref[:, :OFF_B], preferred_element_type=F32)
    zbc = jnp.dot(h_ext, w_ref[:, OFF_B:OFF_D], preferred_element_type=F32)
    row = lax.broadcasted_iota(jnp.int32, (ext, 1), 0)
    keep = jnp.where(row < HALO, keep_p, jnp.where(row >= HALO + TM, keep_n, 1.0))
    staged = ((a_scr, za[:, 0:D_CONV] * jax.nn.sigmoid(za[:, D_CONV:OFF_B])),
              (b_scr, zbc[:, 0:D_POOL]),
              (c_scr, zbc[:, D_POOL + D_SC:D_POOL + 2 * D_SC] * zbc[:, D_POOL + 2 * D_SC:]))
    for scr, val in staged:
        val = val * keep
        for s in range(val.shape[1] // LANES):
            scr[s] = val[:, s * LANES:(s + 1) * LANES]
    c_gate = zbc[HALO:HALO + TM, D_POOL:D_POOL + D_SC]

    h = h_scr[HALO:HALO + TM, :]

    never = pl.program_id(0) < 0
    pace = []
    cw = D_GATE // GATE_CHUNKS
    for ci in range(GATE_CHUNKS):
        cols = slice(ci * cw, (ci + 1) * cw)
        zg = jnp.dot(h, w_ref[:, OFF_G + ci * cw:OFF_G + (ci + 1) * cw], preferred_element_type=F32)
        gate_ref[:, cols] = _half_sigmoid(zg + hbg_ref[:, cols]).astype(BF16)
        pace.append(zg[TM - 1:TM, cw - LANES:cw])

    def paced(step, n_steps, v):
        return jnp.where(never, pace[step * GATE_CHUNKS // n_steps], v)

    def rows(lo, hi):
        return slice(HALO + lo, HALO + TM + hi)

    def depthwise(scr, wt_ref, taps, pacing=None):
        cols = []
        for s in range(scr.shape[0]):
            out = None
            for k in range(taps):
                off = k - taps // 2
                wk = wt_ref[k:k + 1, s * LANES:(s + 1) * LANES]
                if pacing is not None:
                    wk = paced(pacing[0] + s * taps + k, pacing[1], wk)
                term = wk * scr[s, rows(off, off), :]
                out = term if out is None else out + term
            cols.append(out)
        return jnp.concatenate(cols, axis=1)

    n_a = CONV_K * (D_CONV // LANES)
    n_b = D_POOL // LANES
    n_steps = PACE_SPREAD * (n_a + n_b + SC_K * (D_SC // LANES))
    acc = depthwise(a_scr, caw_ref, CONV_K, (0, n_steps)) + cab_ref[...]
    mu = jnp.mean(acc, axis=-1, keepdims=True)
    d = acc - mu
    var = jnp.mean(d * d, axis=-1, keepdims=True)
    a = d * lax.rsqrt(var + EPS) * lng_ref[...] + lnb_ref[...]
    act_ref[:, 0:D_CONV] = (a * jax.nn.sigmoid(a)).astype(BF16)

    lane = lax.broadcasted_iota(jnp.int32, (TM, LANES), 1)
    low_half = lane < POOL_GROUP
    pos = j * TM + lax.broadcasted_iota(jnp.int32, (TM, LANES), 0)
    for s in range(D_POOL // LANES):
        s2_scr, s4_scr = p_scr.at[0, s], p_scr.at[1, s]
        s8_scr = s2_scr
        s2_scr[rows(-7, 7), :] = (paced(n_a + s, n_steps, b_scr[s, rows(-8, 6), :])
                                  + b_scr[s, rows(-7, 7), :])
        s4_scr[rows(-6, 6), :] = s2_scr[rows(-7, 5), :] + s2_scr[rows(-5, 7), :]
        if s == 0:
            wsum = jnp.where(low_half, s2_scr[rows(0, 0), :], s4_scr[rows(0, 0), :])
        else:
            s8_scr[rows(-4, 4), :] = s4_scr[rows(-6, 2), :] + s4_scr[rows(-2, 6), :]
            s16 = s8_scr[rows(-4, -4), :] + s8_scr[rows(4, 4), :]
            wsum = jnp.where(low_half, s8_scr[rows(0, 0), :], s16)
        half = jnp.where(low_half, POOL_WINDOWS[2 * s] // 2, POOL_WINDOWS[2 * s + 1] // 2)
        cnt = (jnp.minimum(pos + half - 1, t - 1) - jnp.maximum(pos - half, 0) + 1).astype(F32)
        act_ref[:, D_CONV + s * LANES:D_CONV + (s + 1) * LANES] = (
            wsum / cnt - b_scr[s, rows(0, 0), :]).astype(BF16)

    act_ref[:, D_CONV + D_POOL:D_ACT] = (
        c_gate * depthwise(c_scr, scw_ref, SC_K, (n_a + n_b, n_steps))).astype(BF16)
    zq_ref[...] = jnp.dot(h, w_ref[:, OFF_D:OFF_G], preferred_element_type=F32).astype(BF16)


def _inproj(x, g, w, hbg, conv_wts, t):
    n = x.shape[0]
    hb = TM // HALO
    nhb = n // HALO
    ext = TM + 2 * HALO
    tok = lambda width: pl.BlockSpec((TM, width), lambda i: (i, 0))
    return pl.pallas_call(
        functools.partial(_inproj_kernel, t=t),
        grid=(n // TM,),
        in_specs=[tok(D_MODEL),
                  pl.BlockSpec((HALO, D_MODEL), lambda i: (jnp.maximum(i * hb - 1, 0), 0)),
                  pl.BlockSpec((HALO, D_MODEL), lambda i: (jnp.minimum((i + 1) * hb, nhb - 1), 0)),
                  _const_spec((1, D_MODEL)),
                  _const_spec(w.shape),
                  _const_spec(hbg.shape)] + [_const_spec(c.shape) for c in conv_wts],
        out_specs=[tok(D_ACT), tok(3 * D_ATT), tok(D_GATE)],
        out_shape=[jax.ShapeDtypeStruct((n, D_ACT), BF16),
                   jax.ShapeDtypeStruct((n, 3 * D_ATT), BF16),
                   jax.ShapeDtypeStruct((n, D_GATE), BF16)],
        scratch_shapes=[pltpu.VMEM((ext, D_MODEL), BF16),
                        pltpu.VMEM((D_CONV // LANES, ext, LANES), F32),
                        pltpu.VMEM((D_POOL // LANES, ext, LANES), F32),
                        pltpu.VMEM((D_SC // LANES, ext, LANES), F32),
                        pltpu.VMEM((2, D_POOL // LANES, ext, LANES), F32)],
        compiler_params=pltpu.CompilerParams(
            dimension_semantics=("parallel",), vmem_limit_bytes=VMEM_LIMIT),
        name="inproj",
    )(x, x, x, g, w, hbg, *conv_wts)


def _attn_kernel(qkv_ref, bias_ref, o_ref, *, rows):
    lane = lax.broadcasted_iota(jnp.int32, (1, D_ATT), 1)
    head_masks = [lane // HEAD_DIM == h for h in range(N_HEADS)]

    def row_body(r, carry):
        rs = jnp.clip(r - WIN_ROWS // 2, 0, rows - WIN_ROWS)
        q0 = pl.multiple_of(r * GRID_W, GRID_W)
        k0 = pl.multiple_of(rs * GRID_W, GRID_W)
        q = qkv_ref[pl.ds(q0, GRID_W), 0:D_ATT]
        k = qkv_ref[pl.ds(k0, WIN_TOKENS), D_ATT:2 * D_ATT]
        v = qkv_ref[pl.ds(k0, WIN_TOKENS), 2 * D_ATT:3 * D_ATT]
        zero = jnp.zeros_like(q)
        qm = jnp.concatenate([jnp.where(m, q, zero) for m in head_masks], axis=0)
        s = lax.dot_general(qm, k, (((1,), (1,)), ((), ())), preferred_element_type=F32)
        s = s + bias_ref[r - rs]
        m = jnp.max(s, axis=-1, keepdims=True)
        p = jnp.exp(s - m)
        l = jnp.sum(p, axis=-1, keepdims=True)
        o = jnp.dot(p.astype(BF16), v, preferred_element_type=F32)
        o = o * (1.0 / l)
        out = jnp.zeros((GRID_W, D_ATT), F32)
        for h, hm in enumerate(head_masks):
            out = jnp.where(hm, o[h * GRID_W:(h + 1) * GRID_W], out)
        o_ref[pl.ds(q0, GRID_W), :] = out.astype(BF16)
        return carry

    lax.fori_loop(0, rows, row_body, 0, unroll=8)


def _attention(zq, bias, t):
    n = zq.shape[0]
    rows = t // GRID_W
    return pl.pallas_call(
        functools.partial(_attn_kernel, rows=rows),
        grid=(n // t,),
        in_specs=[pl.BlockSpec((t, 3 * D_ATT), lambda b: (b, 0)),
                  _const_spec(bias.shape)],
        out_specs=pl.BlockSpec((t, D_ATT), lambda b: (b, 0)),
        out_shape=jax.ShapeDtypeStruct((n, D_ATT), BF16),
        compiler_params=pltpu.CompilerParams(
            dimension_semantics=("parallel",), vmem_limit_bytes=VMEM_LIMIT),
        name="attention",
    )(zq, bias)


def _attn_bias_table(rpb):
    qcol = np.arange(GRID_W)[:, None]
    kcol = np.arange(GRID_W)[None, :]
    cstart = np.clip(qcol - WIN_COLS // 2, 0, GRID_W - WIN_COLS)
    col_mask = (kcol >= cstart) & (kcol < cstart + WIN_COLS)
    col_off = np.clip(kcol - qcol + WIN_COLS - 1, 0, 2 * WIN_COLS - 2)
    b = jnp.where(jnp.asarray(col_mask), rpb.astype(F32)[..., col_off], NEG_INF)
    b = jnp.stack([b[:, WIN_ROWS - 1 - s:2 * WIN_ROWS - 1 - s] for s in range(WIN_ROWS)])
    return b.transpose(0, 1, 3, 2, 4).reshape(WIN_ROWS, N_HEADS * GRID_W, WIN_TOKENS)


def _mixer_kernel(act_ref, att_ref, gate_ref, x_ref, woa_ref, pw_ref, ps_ref, wob_ref, woc_ref,
                  wod_ref, wo_ref, o_ref):
    def proj(v, w_ref):
        return jnp.dot(v, w_ref[...], preferred_element_type=F32)

    pool = proj(act_ref[:, D_CONV:D_CONV + D_POOL], pw_ref) * ps_ref[...]
    branches = (proj(act_ref[:, 0:D_CONV], woa_ref),
                proj(pool.astype(BF16), wob_ref),
                proj(act_ref[:, D_CONV + D_POOL:D_ACT], woc_ref),
                proj(att_ref[...], wod_ref))
    merged = None
    for bi, br in enumerate(branches):
        term = gate_ref[:, bi * D_MODEL:(bi + 1) * D_MODEL].astype(F32) * br
        merged = term if merged is None else merged + term
    o_ref[...] = x_ref[...] + proj(merged.astype(BF16), wo_ref)


def _mixer(act, att, gate, x, wts):
    n = x.shape[0]
    tok = lambda width: pl.BlockSpec((TM, width), lambda i: (i, 0))
    return pl.pallas_call(
        _mixer_kernel,
        grid=(n // TM,),
        in_specs=[tok(D_ACT), tok(D_ATT), tok(D_GATE), tok(D_MODEL)]
        + [_const_spec(w.shape) for w in wts],
        out_specs=tok(D_MODEL),
        out_shape=jax.ShapeDtypeStruct((n, D_MODEL), F32),
        compiler_params=pltpu.CompilerParams(
            dimension_semantics=("parallel",), vmem_limit_bytes=VMEM_LIMIT),
        name="mixer",
    )(act, att, gate, x, *wts)


def _ffn_kernel(xc_ref, xp_ref, xn_ref, g_ref, wup_ref, cw_ref, wdn_ref, fg_ref, o_ref,
                h_scr, u_scr, *, t, final):
    tiles_per_seq = t // TM
    j = pl.program_id(0) % tiles_per_seq
    keep_p = jnp.where(j == 0, 0.0, 1.0)
    keep_n = jnp.where(j == tiles_per_seq - 1, 0.0, 1.0)
    g = g_ref[...]
    x = xc_ref[...]
    h_scr[0:FFN_HALO, :] = (_rms(xp_ref[...], g) * keep_p).astype(BF16)
    h_scr[FFN_HALO:FFN_HALO + TM, :] = _rms(x, g).astype(BF16)
    h_scr[FFN_HALO + TM:FFN_HALO + TM + FFN_HALO, :] = (_rms(xn_ref[...], g) * keep_n).astype(BF16)
    h = h_scr[...]

    def up(c0, c1):
        for off in (0, D_FF):
            u = jnp.dot(h, wup_ref[:, off + c0:off + c1], preferred_element_type=F32)
            for s in range((c1 - c0) // LANES):
                u_scr[(off + c0) // LANES + s] = u[:, s * LANES:(s + 1) * LANES]

    def conv(slab):
        out = None
        for k in range(FFN_K):
            r0 = FFN_HALO - FFN_K // 2 + k
            term = cw_ref[k:k + 1, slab * LANES:(slab + 1) * LANES] * u_scr[slab, r0:r0 + TM, :]
            out = term if out is None else out + term
        return out

    def glu(c0, c1):
        cols = []
        for s in range(c0 // LANES, c1 // LANES):
            val = conv(s)
            hg = conv(D_FF // LANES + s)
            cols.append((val * (hg * (1.0 + jnp.tanh(hg)))).astype(BF16))
        return jnp.concatenate(cols, axis=1)

    acc = x
    up(*FF_CHUNKS[0])
    for ci, (c0, c1) in enumerate(FF_CHUNKS):
        if ci + 1 < len(FF_CHUNKS):
            up(*FF_CHUNKS[ci + 1])
        acc = acc + jnp.dot(glu(c0, c1), wdn_ref[c0:c1, :], preferred_element_type=F32)
    if final:
        acc = _rms(acc, fg_ref[...])
    o_ref[...] = acc


def _ffn(x, g, wup, cw, wdn, fg, t, final):
    n = x.shape[0]
    hb = TM // FFN_HALO
    nhb = n // FFN_HALO
    ext = TM + 2 * FFN_HALO
    return pl.pallas_call(
        functools.partial(_ffn_kernel, t=t, final=final),
        grid=(n // TM,),
        in_specs=[pl.BlockSpec((TM, D_MODEL), lambda i: (i, 0)),
                  pl.BlockSpec((FFN_HALO, D_MODEL), lambda i: (jnp.maximum(i * hb - 1, 0), 0)),
                  pl.BlockSpec((FFN_HALO, D_MODEL), lambda i: (jnp.minimum((i + 1) * hb, nhb - 1), 0)),
                  _const_spec(g.shape), _const_spec(wup.shape), _const_spec(cw.shape),
                  _const_spec(wdn.shape), _const_spec(fg.shape)],
        out_specs=pl.BlockSpec((TM, D_MODEL), lambda i: (i, 0)),
        out_shape=jax.ShapeDtypeStruct((n, D_MODEL), F32),
        scratch_shapes=[pltpu.VMEM((ext, D_MODEL), BF16),
                        pltpu.VMEM((2 * D_FF // LANES, ext, LANES), F32)],
        compiler_params=pltpu.CompilerParams(
            dimension_semantics=("parallel",), vmem_limit_bytes=VMEM_LIMIT),
        name="ffn",
    )(x, x, x, g, wup, cw, wdn, fg)


def _prep_layer(l, norm1_g, w_in, b_gate, conv_a_w, conv_a_b, ln_a_g, ln_a_b, w_out_a, pool_w,
                pool_scale, w_out_b, sc_w, w_out_c, rpb, w_out_d, w_o, norm2_g, w_up, ffn_conv_w,
                w_down):
    row = lambda v: v.reshape(1, -1).astype(F32)
    col_scale = (jnp.ones((w_in.shape[-1],), F32).at[OFF_D:OFF_D + D_ATT].set(HEAD_DIM ** -0.5)
                 .at[OFF_G:].set(0.5))
    up_scale = jnp.ones((w_up.shape[-1],), F32).at[D_FF:].set(0.5)
    pool_bd = jax.scipy.linalg.block_diag(*[pool_w[l, gi] for gi in range(len(POOL_WINDOWS))])
    return dict(
        norm1_g=row(norm1_g[l]),
        w_in=(w_in[l] * col_scale).astype(BF16),
        half_b_gate=row(b_gate[l]) * 0.5,
        bias=_attn_bias_table(rpb[l]),
        conv=(conv_a_w[l].astype(F32), row(conv_a_b[l]), row(ln_a_g[l]), row(ln_a_b[l]),
              sc_w[l].astype(F32)),
        mixer=(w_out_a[l].astype(BF16), pool_bd.astype(BF16), row(pool_scale[l]),
               w_out_b[l].astype(BF16), w_out_c[l].astype(BF16), w_out_d[l].astype(BF16),
               w_o[l].astype(BF16)),
        norm2_g=row(norm2_g[l]),
        w_up=(w_up[l] * up_scale).astype(BF16),
        ffn_conv_w=ffn_conv_w[l].astype(F32),
        w_down=w_down[l].astype(BF16),
    )


def _trunk(x3, layers, final_g):
    b, t, d = x3.shape
    x = x3.reshape(b * t, d)
    fg = final_g.reshape(1, -1).astype(F32)
    for li, p in enumerate(layers):
        act, zq, gate = _inproj(x, p["norm1_g"], p["w_in"], p["half_b_gate"], p["conv"], t)
        att = _attention(zq, p["bias"], t)
        x = _mixer(act, att, gate, x, p["mixer"])
        x = _ffn(x, p["norm2_g"], p["w_up"], p["ffn_conv_w"], p["w_down"], fg, t,
                 final=li == len(layers) - 1)
    return x.reshape(b, t, d)


def kernel(x_prompt, x_sample, norm1_g, w_in, b_gate, conv_a_w, conv_a_b, ln_a_g, ln_a_b, w_out_a,
           pool_w, pool_scale, w_out_b, sc_w, w_out_c, rpb, w_out_d, w_o, norm2_g, w_up, ffn_conv_w,
           w_down, final_g):
    depth = w_in.shape[0]
    layers = [_prep_layer(l, norm1_g, w_in, b_gate, conv_a_w, conv_a_b, ln_a_g, ln_a_b, w_out_a,
                          pool_w, pool_scale, w_out_b, sc_w, w_out_c, rpb, w_out_d, w_o, norm2_g,
                          w_up, ffn_conv_w, w_down)
              for l in range(depth)]
    return _trunk(x_prompt, layers, final_g), _trunk(x_sample, layers, final_g)
```

```python
import functools

import numpy as np
import jax
import jax.numpy as jnp
from jax import lax
from jax.experimental import pallas as pl
from jax.experimental.pallas import tpu as pltpu

F32 = jnp.float32
BF16 = jnp.bfloat16

D_MODEL = 1024
GRID_W = 64
D_CONV = 256
CONV_K = 31
D_POOL = 256
POOL_WINDOWS = (2, 4, 8, 16)
POOL_GROUP = D_POOL // len(POOL_WINDOWS)
D_SC = 256
SC_K = 3
N_HEADS = 4
HEAD_DIM = 64
D_ATT = N_HEADS * HEAD_DIM
WIN_ROWS = 8
WIN_COLS = 16
N_BRANCH = 4
D_FF = 2816
FFN_K = 3
EPS = 1e-6
NEG_INF = -1e30

OFF_B = 2 * D_CONV
OFF_C = OFF_B + D_POOL
OFF_D = OFF_C + 3 * D_SC
OFF_G = OFF_D + 3 * D_ATT
D_GATE = N_BRANCH * D_MODEL
D_ACT = D_CONV + D_POOL + D_SC

WIN_TOKENS = WIN_ROWS * GRID_W
LANES = 128
HALO = 16
GATE_CHUNKS = 16
PACE_SPREAD = 2
FFN_HALO = 8
FF_CHUNKS = ((0, 768), (768, 1536), (1536, 2304), (2304, D_FF))

VMEM_LIMIT = 56 * 1024 * 1024
TM = 512


def _const_spec(shape):
    nd = len(shape)
    return pl.BlockSpec(shape, lambda *_: (0,) * nd, pipeline_mode=pl.Buffered(1))


def _rms(x, g):
    ms = jnp.mean(x * x, axis=-1, keepdims=True)
    return x * lax.rsqrt(ms + EPS) * g


def _half_sigmoid(half_x):
    return 0.5 * jnp.tanh(half_x) + 0.5


def _inproj_kernel(xc_ref, xp_ref, xn_ref, g_ref, w_ref, hbg_ref,
                   caw_ref, cab_ref, lng_ref, lnb_ref, scw_ref,
                   act_ref, zq_ref, gate_ref, h_scr, a_scr, b_scr, c_scr, p_scr, *, t):
    tiles_per_seq = t // TM
    j = pl.program_id(0) % tiles_per_seq
    keep_p = jnp.where(j == 0, 0.0, 1.0)
    keep_n = jnp.where(j == tiles_per_seq - 1, 0.0, 1.0)
    ext = TM + 2 * HALO
    g = g_ref[...]
    h_scr[0:HALO, :] = _rms(xp_ref[...], g).astype(BF16)
    h_scr[HALO:HALO + TM, :] = _rms(xc_ref[...], g).astype(BF16)
    h_scr[HALO + TM:ext, :] = _rms(xn_ref[...], g).astype(BF16)

    h_ext = h_scr[...]
    za = jnp.dot(h_ext, w_ref[:, :OFF_B], preferred_element_type=F32)
    zbc = jnp.dot(h_ext, w_ref[:, OFF_B:OFF_D], preferred_element_type=F32)
    row = lax.broadcasted_iota(jnp.int32, (ext, 1), 0)
    keep = jnp.where(row < HALO, keep_p, jnp.where(row >= HALO + TM, keep_n, 1.0))
    staged = ((a_scr, za[:, 0:D_CONV] * jax.nn.sigmoid(za[:, D_CONV:OFF_B])),
              (b_scr, zbc[:, 0:D_POOL]),
              (c_scr, zbc[:, D_POOL + D_SC:D_POOL + 2 * D_SC] * zbc[:, D_POOL + 2 * D_SC:]))
    for scr, val in staged:
        val = val * keep
        for s in range(val.shape[1] // LANES):
            scr[s] = val[:, s * LANES:(s + 1) * LANES]
    c_gate = zbc[HALO:HALO + TM, D_POOL:D_POOL + D_SC]

    h = h_scr[HALO:HALO + TM, :]

    never = pl.program_id(0) < 0
    pace = []
    cw = D_GATE // GATE_CHUNKS
    for ci in range(GATE_CHUNKS):
        cols = slice(ci * cw, (ci + 1) * cw)
        zg = jnp.dot(h, w_ref[:, OFF_G + ci * cw:OFF_G + (ci + 1) * cw], preferred_element_type=F32)
        gate_ref[:, cols] = _half_sigmoid(zg + hbg_ref[:, cols]).astype(BF16)
        pace.append(zg[TM - 1:TM, cw - LANES:cw])

    def paced(step, n_steps, v):
        return jnp.where(never, pace[step * GATE_CHUNKS // n_steps], v)

    def rows(lo, hi):
        return slice(HALO + lo, HALO + TM + hi)

    def depthwise(scr, wt_ref, taps, pacing=None):
        cols = []
        for s in range(scr.shape[0]):
            out = None
            for k in range(taps):
                off = k - taps // 2
                wk = wt_ref[k:k + 1, s * LANES:(s + 1) * LANES]
                if pacing is not None:
                    wk = paced(pacing[0] + s * taps + k, pacing[1], wk)
                term = wk * scr[s, rows(off, off), :]
                out = term if out is None else out + term
            cols.append(out)
        return jnp.concatenate(cols, axis=1)

    n_a = CONV_K * (D_CONV // LANES)
    n_b = D_POOL // LANES
    n_steps = PACE_SPREAD * (n_a + n_b + SC_K * (D_SC // LANES))
    acc = depthwise(a_scr, caw_ref, CONV_K, (0, n_steps)) + cab_ref[...]
    mu = jnp.mean(acc, axis=-1, keepdims=True)
    d = acc - mu
    var = jnp.mean(d * d, axis=-1, keepdims=True)
    a = d * lax.rsqrt(var + EPS) * lng_ref[...] + lnb_ref[...]
    act_ref[:, 0:D_CONV] = (a * jax.nn.sigmoid(a)).astype(BF16)

    lane = lax.broadcasted_iota(jnp.int32, (TM, LANES), 1)
    low_half = lane < POOL_GROUP
    pos = j * TM + lax.broadcasted_iota(jnp.int32, (TM, LANES), 0)
    for s in range(D_POOL // LANES):
        s2_scr, s4_scr = p_scr.at[0, s], p_scr.at[1, s]
        s8_scr = s2_scr
        s2_scr[rows(-7, 7), :] = (paced(n_a + s, n_steps, b_scr[s, rows(-8, 6), :])
                                  + b_scr[s, rows(-7, 7), :])
        s4_scr[rows(-6, 6), :] = s2_scr[rows(-7, 5), :] + s2_scr[rows(-5, 7), :]
        if s == 0:
            wsum = jnp.where(low_half, s2_scr[rows(0, 0), :], s4_scr[rows(0, 0), :])
        else:
            s8_scr[rows(-4, 4), :] = s4_scr[rows(-6, 2), :] + s4_scr[rows(-2, 6), :]
            s16 = s8_scr[rows(-4, -4), :] + s8_scr[rows(4, 4), :]
            wsum = jnp.where(low_half, s8_scr[rows(0, 0), :], s16)
        half = jnp.where(low_half, POOL_WINDOWS[2 * s] // 2, POOL_WINDOWS[2 * s + 1] // 2)
        cnt = (jnp.minimum(pos + half - 1, t - 1) - jnp.maximum(pos - half, 0) + 1).astype(F32)
        act_ref[:, D_CONV + s * LANES:D_CONV + (s + 1) * LANES] = (
            wsum / cnt - b_scr[s, rows(0, 0), :]).astype(BF16)

    act_ref[:, D_CONV + D_POOL:D_ACT] = (
        c_gate * depthwise(c_scr, scw_ref, SC_K, (n_a + n_b, n_steps))).astype(BF16)
    zq_ref[...] = jnp.dot(h, w_ref[:, OFF_D:OFF_G], preferred_element_type=F32).astype(BF16)


def _inproj(x, g, w, hbg, conv_wts, t):
    n = x.shape[0]
    hb = TM // HALO
    nhb = n // HALO
    ext = TM + 2 * HALO
    tok = lambda width: pl.BlockSpec((TM, width), lambda i: (i, 0))
    return pl.pallas_call(
        functools.partial(_inproj_kernel, t=t),
        grid=(n // TM,),
        in_specs=[tok(D_MODEL),
                  pl.BlockSpec((HALO, D_MODEL), lambda i: (jnp.maximum(i * hb - 1, 0), 0)),
                  pl.BlockSpec((HALO, D_MODEL), lambda i: (jnp.minimum((i + 1) * hb, nhb - 1), 0)),
                  _const_spec((1, D_MODEL)),
                  _const_spec(w.shape),
                  _const_spec(hbg.shape)] + [_const_spec(c.shape) for c in conv_wts],
        out_specs=[tok(D_ACT), tok(3 * D_ATT), tok(D_GATE)],
        out_shape=[jax.ShapeDtypeStruct((n, D_ACT), BF16),
                   jax.ShapeDtypeStruct((n, 3 * D_ATT), BF16),
                   jax.ShapeDtypeStruct((n, D_GATE), BF16)],
        scratch_shapes=[pltpu.VMEM((ext, D_MODEL), BF16),
                        pltpu.VMEM((D_CONV // LANES, ext, LANES), F32),
                        pltpu.VMEM((D_POOL // LANES, ext, LANES), F32),
                        pltpu.VMEM((D_SC // LANES, ext, LANES), F32),
                        pltpu.VMEM((2, D_POOL // LANES, ext, LANES), F32)],
        compiler_params=pltpu.CompilerParams(
            dimension_semantics=("parallel",), vmem_limit_bytes=VMEM_LIMIT),
        name="inproj",
    )(x, x, x, g, w, hbg, *conv_wts)


def _attention_rows(q_ref, kv_scr, bias_ref, att_scr, first_row, rows):
    lane = lax.broadcasted_iota(jnp.int32, (1, D_ATT), 1)
    head_masks = [lane // HEAD_DIM == h for h in range(N_HEADS)]
    for r in range(TM // GRID_W):
        row = first_row + r
        rs = jnp.clip(row - WIN_ROWS // 2, 0, rows - WIN_ROWS)
        k0 = pl.multiple_of((rs - first_row) * GRID_W + TM, GRID_W)
        q = q_ref[r * GRID_W:(r + 1) * GRID_W, :]
        k = kv_scr[pl.ds(k0, WIN_TOKENS), 0:D_ATT]
        v = kv_scr[pl.ds(k0, WIN_TOKENS), D_ATT:2 * D_ATT]
        zero = jnp.zeros_like(q)
        qm = jnp.concatenate([jnp.where(m, q, zero) for m in head_masks], axis=0)
        s = lax.dot_general(qm, k, (((1,), (1,)), ((), ())), preferred_element_type=F32)
        s = s + bias_ref[row - rs]
        m = jnp.max(s, axis=-1, keepdims=True)
        p = jnp.exp(s - m)
        l = jnp.sum(p, axis=-1, keepdims=True)
        o = jnp.dot(p.astype(BF16), v, preferred_element_type=F32)
        o = o * (1.0 / l)
        out = jnp.zeros((GRID_W, D_ATT), F32)
        for h, hm in enumerate(head_masks):
            out = jnp.where(hm, o[h * GRID_W:(h + 1) * GRID_W], out)
        att_scr[r * GRID_W:(r + 1) * GRID_W, :] = out.astype(BF16)


def _attn_bias_table(rpb):
    qcol = np.arange(GRID_W)[:, None]
    kcol = np.arange(GRID_W)[None, :]
    cstart = np.clip(qcol - WIN_COLS // 2, 0, GRID_W - WIN_COLS)
    col_mask = (kcol >= cstart) & (kcol < cstart + WIN_COLS)
    col_off = np.clip(kcol - qcol + WIN_COLS - 1, 0, 2 * WIN_COLS - 2)
    b = jnp.where(jnp.asarray(col_mask), rpb.astype(F32)[..., col_off], NEG_INF)
    b = jnp.stack([b[:, WIN_ROWS - 1 - s:2 * WIN_ROWS - 1 - s] for s in range(WIN_ROWS)])
    return b.transpose(0, 1, 3, 2, 4).reshape(WIN_ROWS, N_HEADS * GRID_W, WIN_TOKENS)


def _mixer_kernel(q_ref, kvp_ref, kvc_ref, kvn_ref, bias_ref, act_ref, gate_ref, x_ref,
                  woa_ref, pw_ref, ps_ref, wob_ref, woc_ref, wod_ref, wo_ref, o_ref,
                  kv_scr, att_scr, *, t):
    kv_scr[0:TM, :] = kvp_ref[...]
    kv_scr[TM:2 * TM, :] = kvc_ref[...]
    kv_scr[2 * TM:3 * TM, :] = kvn_ref[...]
    first_row = (pl.program_id(0) % (t // TM)) * (TM // GRID_W)
    _attention_rows(q_ref, kv_scr, bias_ref, att_scr, first_row, t // GRID_W)

    def proj(v, w_ref):
        return jnp.dot(v, w_ref[...], preferred_element_type=F32)

    pool = proj(act_ref[:, D_CONV:D_CONV + D_POOL], pw_ref) * ps_ref[...]
    branches = (proj(act_ref[:, 0:D_CONV], woa_ref),
                proj(pool.astype(BF16), wob_ref),
                proj(act_ref[:, D_CONV + D_POOL:D_ACT], woc_ref),
                proj(att_scr[...], wod_ref))
    merged = None
    for bi, br in enumerate(branches):
        term = gate_ref[:, bi * D_MODEL:(bi + 1) * D_MODEL].astype(F32) * br
        merged = term if merged is None else merged + term
    o_ref[...] = x_ref[...] + proj(merged.astype(BF16), wo_ref)


def _mixer(zq, bias, act, gate, x, wts, t):
    n = x.shape[0]
    nt = n // TM
    tok = lambda width, col=0: pl.BlockSpec((TM, width), lambda i: (i, col))
    kv = lambda index: pl.BlockSpec((TM, 2 * D_ATT), lambda i: (index(i), 0))
    return pl.pallas_call(
        functools.partial(_mixer_kernel, t=t),
        grid=(nt,),
        in_specs=[tok(D_ATT, 2),
                  kv(lambda i: jnp.maximum(i - 1, 0)), kv(lambda i: i),
                  kv(lambda i: jnp.minimum(i + 1, nt - 1)),
                  _const_spec(bias.shape),
                  tok(D_ACT), tok(D_GATE), tok(D_MODEL)]
        + [_const_spec(w.shape) for w in wts],
        out_specs=tok(D_MODEL),
        out_shape=jax.ShapeDtypeStruct((n, D_MODEL), F32),
        scratch_shapes=[pltpu.VMEM((3 * TM, 2 * D_ATT), BF16),
                        pltpu.VMEM((TM, D_ATT), BF16)],
        compiler_params=pltpu.CompilerParams(
            dimension_semantics=("parallel",), vmem_limit_bytes=VMEM_LIMIT),
        name="mixer",
    )(zq, zq, zq, zq, bias, act, gate, x, *wts)


def _ffn_kernel(xc_ref, xp_ref, xn_ref, g_ref, wup_ref, cw_ref, wdn_ref, fg_ref, o_ref,
                h_scr, u_scr, *, t, final):
    tiles_per_seq = t // TM
    j = pl.program_id(0) % tiles_per_seq
    keep_p = jnp.where(j == 0, 0.0, 1.0)
    keep_n = jnp.where(j == tiles_per_seq - 1, 0.0, 1.0)
    g = g_ref[...]
    x = xc_ref[...]
    h_scr[0:FFN_HALO, :] = (_rms(xp_ref[...], g) * keep_p).astype(BF16)
    h_scr[FFN_HALO:FFN_HALO + TM, :] = _rms(x, g).astype(BF16)
    h_scr[FFN_HALO + TM:FFN_HALO + TM + FFN_HALO, :] = (_rms(xn_ref[...], g) * keep_n).astype(BF16)
    h = h_scr[...]

    def up(c0, c1):
        for off in (0, D_FF):
            u = jnp.dot(h, wup_ref[:, off + c0:off + c1], preferred_element_type=F32)
            for s in range((c1 - c0) // LANES):
                u_scr[(off + c0) // LANES + s] = u[:, s * LANES:(s + 1) * LANES]

    def conv(slab):
        out = None
        for k in range(FFN_K):
            r0 = FFN_HALO - FFN_K // 2 + k
            term = cw_ref[k:k + 1, slab * LANES:(slab + 1) * LANES] * u_scr[slab, r0:r0 + TM, :]
            out = term if out is None else out + term
        return out

    def glu(c0, c1):
        cols = []
        for s in range(c0 // LANES, c1 // LANES):
            val = conv(s)
            hg = conv(D_FF // LANES + s)
            cols.append((val * (hg * (1.0 + jnp.tanh(hg)))).astype(BF16))
        return jnp.concatenate(cols, axis=1)

    acc = x
    up(*FF_CHUNKS[0])
    for ci, (c0, c1) in enumerate(FF_CHUNKS):
        if ci + 1 < len(FF_CHUNKS):
            up(*FF_CHUNKS[ci + 1])
        acc = acc + jnp.dot(glu(c0, c1), wdn_ref[c0:c1, :], preferred_element_type=F32)
    if final:
        acc = _rms(acc, fg_ref[...])
    o_ref[...] = acc


def _ffn(x, g, wup, cw, wdn, fg, t, final):
    n = x.shape[0]
    hb = TM // FFN_HALO
    nhb = n // FFN_HALO
    ext = TM + 2 * FFN_HALO
    return pl.pallas_call(
        functools.partial(_ffn_kernel, t=t, final=final),
        grid=(n // TM,),
        in_specs=[pl.BlockSpec((TM, D_MODEL), lambda i: (i, 0)),
                  pl.BlockSpec((FFN_HALO, D_MODEL), lambda i: (jnp.maximum(i * hb - 1, 0), 0)),
                  pl.BlockSpec((FFN_HALO, D_MODEL), lambda i: (jnp.minimum((i + 1) * hb, nhb - 1), 0)),
                  _const_spec(g.shape), _const_spec(wup.shape), _const_spec(cw.shape),
                  _const_spec(wdn.shape), _const_spec(fg.shape)],
        out_specs=pl.BlockSpec((TM, D_MODEL), lambda i: (i, 0)),
        out_shape=jax.ShapeDtypeStruct((n, D_MODEL), F32),
        scratch_shapes=[pltpu.VMEM((ext, D_MODEL), BF16),
                        pltpu.VMEM((2 * D_FF // LANES, ext, LANES), F32)],
        compiler_params=pltpu.CompilerParams(
            dimension_semantics=("parallel",), vmem_limit_bytes=VMEM_LIMIT),
        name="ffn",
    )(x, x, x, g, wup, cw, wdn, fg)


def _prep_layer(l, norm1_g, w_in, b_gate, conv_a_w, conv_a_b, ln_a_g, ln_a_b, w_out_a, pool_w,
                pool_scale, w_out_b, sc_w, w_out_c, rpb, w_out_d, w_o, norm2_g, w_up, ffn_conv_w,
                w_down):
    row = lambda v: v.reshape(1, -1).astype(F32)
    col_scale = (jnp.ones((w_in.shape[-1],), F32).at[OFF_D:OFF_D + D_ATT].set(HEAD_DIM ** -0.5)
                 .at[OFF_G:].set(0.5))
    up_scale = jnp.ones((w_up.shape[-1],), F32).at[D_FF:].set(0.5)
    pool_bd = jax.scipy.linalg.block_diag(*[pool_w[l, gi] for gi in range(len(POOL_WINDOWS))])
    w = (w_in[l] * col_scale).astype(BF16)
    return dict(
        norm1_g=row(norm1_g[l]),
        w_in=jnp.concatenate([w[:, :OFF_D], w[:, OFF_D + D_ATT:OFF_G], w[:, OFF_D:OFF_D + D_ATT],
                              w[:, OFF_G:]], axis=1),
        half_b_gate=row(b_gate[l]) * 0.5,
        bias=_attn_bias_table(rpb[l]),
        conv=(conv_a_w[l].astype(F32), row(conv_a_b[l]), row(ln_a_g[l]), row(ln_a_b[l]),
              sc_w[l].astype(F32)),
        mixer=(w_out_a[l].astype(BF16), pool_bd.astype(BF16), row(pool_scale[l]),
               w_out_b[l].astype(BF16), w_out_c[l].astype(BF16), w_out_d[l].astype(BF16),
               w_o[l].astype(BF16)),
        norm2_g=row(norm2_g[l]),
        w_up=(w_up[l] * up_scale).astype(BF16),
        ffn_conv_w=ffn_conv_w[l].astype(F32),
        w_down=w_down[l].astype(BF16),
    )


def _trunk(x3, layers, final_g):
    b, t, d = x3.shape
    x = x3.reshape(b * t, d)
    fg = final_g.reshape(1, -1).astype(F32)
    for li, p in enumerate(layers):
        act, zq, gate = _inproj(x, p["norm1_g"], p["w_in"], p["half_b_gate"], p["conv"], t)
        x = _mixer(zq, p["bias"], act, gate, x, p["mixer"], t)
        x = _ffn(x, p["norm2_g"], p["w_up"], p["ffn_conv_w"], p["w_down"], fg, t,
                 final=li == len(layers) - 1)
    return x.reshape(b, t, d)


def kernel(x_prompt, x_sample, norm1_g, w_in, b_gate, conv_a_w, conv_a_b, ln_a_g, ln_a_b, w_out_a,
           pool_w, pool_scale, w_out_b, sc_w, w_out_c, rpb, w_out_d, w_o, norm2_g, w_up, ffn_conv_w,
           w_down, final_g):
    depth = w_in.shape[0]
    layers = [_prep_layer(l, norm1_g, w_in, b_gate, conv_a_w, conv_a_b, ln_a_g, ln_a_b, w_out_a,
                          pool_w, pool_scale, w_out_b, sc_w, w_out_c, rpb, w_out_d, w_o, norm2_g,
                          w_up, ffn_conv_w, w_down)
              for l in range(depth)]
    return _trunk(x_prompt, layers, final_g), _trunk(x_sample, layers, final_g)
```

```python
import functools

import numpy as np
import jax
import jax.numpy as jnp
from jax import lax
from jax.experimental import pallas as pl
from jax.experimental.pallas import tpu as pltpu

F32 = jnp.float32
BF16 = jnp.bfloat16

D_MODEL = 1024
GRID_W = 64
D_CONV = 256
CONV_K = 31
D_POOL = 256
POOL_WINDOWS = (2, 4, 8, 16)
POOL_GROUP = D_POOL // len(POOL_WINDOWS)
D_SC = 256
SC_K = 3
N_HEADS = 4
HEAD_DIM = 64
D_ATT = N_HEADS * HEAD_DIM
WIN_ROWS = 8
WIN_COLS = 16
N_BRANCH = 4
D_FF = 2816
FFN_K = 3
EPS = 1e-6
NEG_INF = -1e30

OFF_B = 2 * D_CONV
OFF_C = OFF_B + D_POOL
OFF_D = OFF_C + 3 * D_SC
OFF_G = OFF_D + 3 * D_ATT
D_GATE = N_BRANCH * D_MODEL
D_ACT = D_CONV + D_POOL + D_SC

WIN_TOKENS = WIN_ROWS * GRID_W
LANES = 128
HALO = 16
GATE_CHUNKS = 16
CONV_ROWS = 64
PACE_CHUNKS = 16
FFN_HALO = 8
FF_CHUNKS = ((0, 768), (768, 1536), (1536, 2304), (2304, D_FF))

VMEM_LIMIT = 56 * 1024 * 1024
TM = 512


def _const_spec(shape):
    nd = len(shape)
    return pl.BlockSpec(shape, lambda *_: (0,) * nd, pipeline_mode=pl.Buffered(1))


def _rms(x, g):
    ms = jnp.mean(x * x, axis=-1, keepdims=True)
    return x * lax.rsqrt(ms + EPS) * g


def _half_sigmoid(half_x):
    return 0.5 * jnp.tanh(half_x) + 0.5


def _inproj_kernel(xc_ref, xp_ref, xn_ref, g_ref, w_ref, hbg_ref,
                   caw_ref, cab_ref, lng_ref, lnb_ref, scw_ref,
                   act_ref, zq_ref, gate_ref, h_scr, a_scr, b_scr, c_scr, p_scr, *, t):
    tiles_per_seq = t // TM
    j = pl.program_id(0) % tiles_per_seq
    keep_p = jnp.where(j == 0, 0.0, 1.0)
    keep_n = jnp.where(j == tiles_per_seq - 1, 0.0, 1.0)
    ext = TM + 2 * HALO
    g = g_ref[...]
    h_scr[0:HALO, :] = _rms(xp_ref[...], g).astype(BF16)
    h_scr[HALO:HALO + TM, :] = _rms(xc_ref[...], g).astype(BF16)
    h_scr[HALO + TM:ext, :] = _rms(xn_ref[...], g).astype(BF16)

    h_ext = h_scr[...]
    zbc = jnp.dot(h_ext, w_ref[:, OFF_B:OFF_D], preferred_element_type=F32)
    za = jnp.dot(h_ext, w_ref[:, :OFF_B], preferred_element_type=F32)
    row = lax.broadcasted_iota(jnp.int32, (ext, 1), 0)
    keep = jnp.where(row < HALO, keep_p, jnp.where(row >= HALO + TM, keep_n, 1.0))
    staged = ((a_scr, za[:, 0:D_CONV] * jax.nn.sigmoid(za[:, D_CONV:OFF_B])),
              (b_scr, zbc[:, 0:D_POOL]),
              (c_scr, zbc[:, D_POOL + D_SC:D_POOL + 2 * D_SC] * zbc[:, D_POOL + 2 * D_SC:]))
    for scr, val in staged:
        val = val * keep
        for s in range(val.shape[1] // LANES):
            scr[s] = val[:, s * LANES:(s + 1) * LANES]
    c_gate = zbc[HALO:HALO + TM, D_POOL:D_POOL + D_SC]

    h = h_scr[HALO:HALO + TM, :]

    never = pl.program_id(0) < 0
    pace = []
    cw = D_GATE // GATE_CHUNKS
    for ci in range(GATE_CHUNKS):
        cols = slice(ci * cw, (ci + 1) * cw)
        zg = jnp.dot(h, w_ref[:, OFF_G + ci * cw:OFF_G + (ci + 1) * cw], preferred_element_type=F32)
        gate_ref[:, cols] = _half_sigmoid(zg + hbg_ref[:, cols]).astype(BF16)
        pace.append(zg[TM - 1:TM, cw - LANES:cw])

    def paced(step, n_steps, v):
        return jnp.where(never, pace[step * PACE_CHUNKS // n_steps], v)

    def rows(lo, hi):
        return slice(HALO + lo, HALO + TM + hi)

    def depthwise(scr, wt_ref, taps, r0, nr, pacing):
        cols = []
        for s in range(scr.shape[0]):
            out = None
            for k in range(taps):
                start = HALO + r0 + k - taps // 2
                wk = wt_ref[k:k + 1, s * LANES:(s + 1) * LANES]
                if pacing is not None:
                    wk = paced(pacing[0] + s * taps + k, pacing[1], wk)
                term = wk * scr[s, start:start + nr, :]
                out = term if out is None else out + term
            cols.append(out)
        return jnp.concatenate(cols, axis=1)

    n_blk = CONV_K * (D_CONV // LANES)
    n_steps = (TM // CONV_ROWS) * n_blk
    for rb in range(TM // CONV_ROWS):
        r0 = rb * CONV_ROWS
        acc = depthwise(a_scr, caw_ref, CONV_K, r0, CONV_ROWS, (rb * n_blk, n_steps)) + cab_ref[...]
        mu = jnp.mean(acc, axis=-1, keepdims=True)
        d = acc - mu
        var = jnp.mean(d * d, axis=-1, keepdims=True)
        a = d * lax.rsqrt(var + EPS) * lng_ref[...] + lnb_ref[...]
        act_ref[r0:r0 + CONV_ROWS, 0:D_CONV] = (a * jax.nn.sigmoid(a)).astype(BF16)

    lane = lax.broadcasted_iota(jnp.int32, (TM, LANES), 1)
    low_half = lane < POOL_GROUP
    pos = j * TM + lax.broadcasted_iota(jnp.int32, (TM, LANES), 0)
    for s in range(D_POOL // LANES):
        s2_scr, s4_scr = p_scr.at[0, s], p_scr.at[1, s]
        s8_scr = s2_scr
        s2_scr[rows(-7, 7), :] = b_scr[s, rows(-8, 6), :] + b_scr[s, rows(-7, 7), :]
        s4_scr[rows(-6, 6), :] = s2_scr[rows(-7, 5), :] + s2_scr[rows(-5, 7), :]
        if s == 0:
            wsum = jnp.where(low_half, s2_scr[rows(0, 0), :], s4_scr[rows(0, 0), :])
        else:
            s8_scr[rows(-4, 4), :] = s4_scr[rows(-6, 2), :] + s4_scr[rows(-2, 6), :]
            s16 = s8_scr[rows(-4, -4), :] + s8_scr[rows(4, 4), :]
            wsum = jnp.where(low_half, s8_scr[rows(0, 0), :], s16)
        half = jnp.where(low_half, POOL_WINDOWS[2 * s] // 2, POOL_WINDOWS[2 * s + 1] // 2)
        cnt = (jnp.minimum(pos + half - 1, t - 1) - jnp.maximum(pos - half, 0) + 1).astype(F32)
        act_ref[:, D_CONV + s * LANES:D_CONV + (s + 1) * LANES] = (
            wsum / cnt - b_scr[s, rows(0, 0), :]).astype(BF16)

    act_ref[:, D_CONV + D_POOL:D_ACT] = (
        c_gate * depthwise(c_scr, scw_ref, SC_K, 0, TM, None)).astype(BF16)
    zq_ref[...] = jnp.dot(h, w_ref[:, OFF_D:OFF_G], preferred_element_type=F32).astype(BF16)


def _inproj(x, g, w, hbg, conv_wts, t):
    n = x.shape[0]
    hb = TM // HALO
    nhb = n // HALO
    ext = TM + 2 * HALO
    tok = lambda width: pl.BlockSpec((TM, width), lambda i: (i, 0))
    return pl.pallas_call(
        functools.partial(_inproj_kernel, t=t),
        grid=(n // TM,),
        in_specs=[tok(D_MODEL),
                  pl.BlockSpec((HALO, D_MODEL), lambda i: (jnp.maximum(i * hb - 1, 0), 0)),
                  pl.BlockSpec((HALO, D_MODEL), lambda i: (jnp.minimum((i + 1) * hb, nhb - 1), 0)),
                  _const_spec((1, D_MODEL)),
                  _const_spec(w.shape),
                  _const_spec(hbg.shape)] + [_const_spec(c.shape) for c in conv_wts],
        out_specs=[tok(D_ACT), tok(3 * D_ATT), tok(D_GATE)],
        out_shape=[jax.ShapeDtypeStruct((n, D_ACT), BF16),
                   jax.ShapeDtypeStruct((n, 3 * D_ATT), BF16),
                   jax.ShapeDtypeStruct((n, D_GATE), BF16)],
        scratch_shapes=[pltpu.VMEM((ext, D_MODEL), BF16),
                        pltpu.VMEM((D_CONV // LANES, ext, LANES), F32),
                        pltpu.VMEM((D_POOL // LANES, ext, LANES), F32),
                        pltpu.VMEM((D_SC // LANES, ext, LANES), F32),
                        pltpu.VMEM((2, D_POOL // LANES, ext, LANES), F32)],
        compiler_params=pltpu.CompilerParams(
            dimension_semantics=("parallel",), vmem_limit_bytes=VMEM_LIMIT),
        name="inproj",
    )(x, x, x, g, w, hbg, *conv_wts)


def _attention_rows(q_ref, kv_scr, bias_ref, att_scr, first_row, rows):
    lane = lax.broadcasted_iota(jnp.int32, (1, D_ATT), 1)
    head_masks = [lane // HEAD_DIM == h for h in range(N_HEADS)]
    for r in range(TM // GRID_W):
        row = first_row + r
        rs = jnp.clip(row - WIN_ROWS // 2, 0, rows - WIN_ROWS)
        k0 = pl.multiple_of((rs - first_row) * GRID_W + TM, GRID_W)
        q = q_ref[r * GRID_W:(r + 1) * GRID_W, :]
        k = kv_scr[pl.ds(k0, WIN_TOKENS), 0:D_ATT]
        v = kv_scr[pl.ds(k0, WIN_TOKENS), D_ATT:2 * D_ATT]
        zero = jnp.zeros_like(q)
        qm = jnp.concatenate([jnp.where(m, q, zero) for m in head_masks], axis=0)
        s = lax.dot_general(qm, k, (((1,), (1,)), ((), ())), preferred_element_type=F32)
        s = s + bias_ref[row - rs]
        m = jnp.max(s, axis=-1, keepdims=True)
        p = jnp.exp(s - m)
        l = jnp.sum(p, axis=-1, keepdims=True)
        o = jnp.dot(p.astype(BF16), v, preferred_element_type=F32)
        o = o * (1.0 / l)
        out = jnp.zeros((GRID_W, D_ATT), F32)
        for h, hm in enumerate(head_masks):
            out = jnp.where(hm, o[h * GRID_W:(h + 1) * GRID_W], out)
        att_scr[r * GRID_W:(r + 1) * GRID_W, :] = out.astype(BF16)


def _attn_bias_table(rpb):
    qcol = np.arange(GRID_W)[:, None]
    kcol = np.arange(GRID_W)[None, :]
    cstart = np.clip(qcol - WIN_COLS // 2, 0, GRID_W - WIN_COLS)
    col_mask = (kcol >= cstart) & (kcol < cstart + WIN_COLS)
    col_off = np.clip(kcol - qcol + WIN_COLS - 1, 0, 2 * WIN_COLS - 2)
    b = jnp.where(jnp.asarray(col_mask), rpb.astype(F32)[..., col_off], NEG_INF)
    b = jnp.stack([b[:, WIN_ROWS - 1 - s:2 * WIN_ROWS - 1 - s] for s in range(WIN_ROWS)])
    return b.transpose(0, 1, 3, 2, 4).reshape(WIN_ROWS, N_HEADS * GRID_W, WIN_TOKENS)


def _mixer_kernel(q_ref, kvp_ref, kvc_ref, kvn_ref, bias_ref, act_ref, gate_ref, x_ref,
                  woa_ref, pw_ref, ps_ref, wob_ref, woc_ref, wod_ref, wo_ref, o_ref,
                  kv_scr, att_scr, *, t):
    kv_scr[0:TM, :] = kvp_ref[...]
    kv_scr[TM:2 * TM, :] = kvc_ref[...]
    kv_scr[2 * TM:3 * TM, :] = kvn_ref[...]
    first_row = (pl.program_id(0) % (t // TM)) * (TM // GRID_W)
    _attention_rows(q_ref, kv_scr, bias_ref, att_scr, first_row, t // GRID_W)

    def proj(v, w_ref):
        return jnp.dot(v, w_ref[...], preferred_element_type=F32)

    pool = proj(act_ref[:, D_CONV:D_CONV + D_POOL], pw_ref) * ps_ref[...]
    branches = (proj(act_ref[:, 0:D_CONV], woa_ref),
                proj(pool.astype(BF16), wob_ref),
                proj(act_ref[:, D_CONV + D_POOL:D_ACT], woc_ref),
                proj(att_scr[...], wod_ref))
    merged = None
    for bi, br in enumerate(branches):
        term = gate_ref[:, bi * D_MODEL:(bi + 1) * D_MODEL].astype(F32) * br
        merged = term if merged is None else merged + term
    o_ref[...] = x_ref[...] + proj(merged.astype(BF16), wo_ref)


def _mixer(zq, bias, act, gate, x, wts, t):
    n = x.shape[0]
    nt = n // TM
    tok = lambda width, col=0: pl.BlockSpec((TM, width), lambda i: (i, col))
    kv = lambda index: pl.BlockSpec((TM, 2 * D_ATT), lambda i: (index(i), 0))
    return pl.pallas_call(
        functools.partial(_mixer_kernel, t=t),
        grid=(nt,),
        in_specs=[tok(D_ATT, 2),
                  kv(lambda i: jnp.maximum(i - 1, 0)), kv(lambda i: i),
                  kv(lambda i: jnp.minimum(i + 1, nt - 1)),
                  _const_spec(bias.shape),
                  tok(D_ACT), tok(D_GATE), tok(D_MODEL)]
        + [_const_spec(w.shape) for w in wts],
        out_specs=tok(D_MODEL),
        out_shape=jax.ShapeDtypeStruct((n, D_MODEL), F32),
        scratch_shapes=[pltpu.VMEM((3 * TM, 2 * D_ATT), BF16),
                        pltpu.VMEM((TM, D_ATT), BF16)],
        compiler_params=pltpu.CompilerParams(
            dimension_semantics=("parallel",), vmem_limit_bytes=VMEM_LIMIT),
        name="mixer",
    )(zq, zq, zq, zq, bias, act, gate, x, *wts)


def _ffn_kernel(xc_ref, xp_ref, xn_ref, g_ref, wup_ref, cw_ref, wdn_ref, fg_ref, o_ref,
                h_scr, u_scr, *, t, final):
    tiles_per_seq = t // TM
    j = pl.program_id(0) % tiles_per_seq
    keep_p = jnp.where(j == 0, 0.0, 1.0)
    keep_n = jnp.where(j == tiles_per_seq - 1, 0.0, 1.0)
    g = g_ref[...]
    x = xc_ref[...]
    h_scr[0:FFN_HALO, :] = (_rms(xp_ref[...], g) * keep_p).astype(BF16)
    h_scr[FFN_HALO:FFN_HALO + TM, :] = _rms(x, g).astype(BF16)
    h_scr[FFN_HALO + TM:FFN_HALO + TM + FFN_HALO, :] = (_rms(xn_ref[...], g) * keep_n).astype(BF16)
    h = h_scr[...]

    def up(c0, c1):
        for off in (0, D_FF):
            u = jnp.dot(h, wup_ref[:, off + c0:off + c1], preferred_element_type=F32)
            for s in range((c1 - c0) // LANES):
                u_scr[(off + c0) // LANES + s] = u[:, s * LANES:(s + 1) * LANES]

    def conv(slab):
        out = None
        for k in range(FFN_K):
            r0 = FFN_HALO - FFN_K // 2 + k
            term = cw_ref[k:k + 1, slab * LANES:(slab + 1) * LANES] * u_scr[slab, r0:r0 + TM, :]
            out = term if out is None else out + term
        return out

    def glu(c0, c1):
        cols = []
        for s in range(c0 // LANES, c1 // LANES):
            val = conv(s)
            hg = conv(D_FF // LANES + s)
            cols.append((val * (hg * (1.0 + jnp.tanh(hg)))).astype(BF16))
        return jnp.concatenate(cols, axis=1)

    acc = x
    up(*FF_CHUNKS[0])
    for ci, (c0, c1) in enumerate(FF_CHUNKS):
        if ci + 1 < len(FF_CHUNKS):
            up(*FF_CHUNKS[ci + 1])
        acc = acc + jnp.dot(glu(c0, c1), wdn_ref[c0:c1, :], preferred_element_type=F32)
    if final:
        acc = _rms(acc, fg_ref[...])
    o_ref[...] = acc


def _ffn(x, g, wup, cw, wdn, fg, t, final):
    n = x.shape[0]
    hb = TM // FFN_HALO
    nhb = n // FFN_HALO
    ext = TM + 2 * FFN_HALO
    return pl.pallas_call(
        functools.partial(_ffn_kernel, t=t, final=final),
        grid=(n // TM,),
        in_specs=[pl.BlockSpec((TM, D_MODEL), lambda i: (i, 0)),
                  pl.BlockSpec((FFN_HALO, D_MODEL), lambda i: (jnp.maximum(i * hb - 1, 0), 0)),
                  pl.BlockSpec((FFN_HALO, D_MODEL), lambda i: (jnp.minimum((i + 1) * hb, nhb - 1), 0)),
                  _const_spec(g.shape), _const_spec(wup.shape), _const_spec(cw.shape),
                  _const_spec(wdn.shape), _const_spec(fg.shape)],
        out_specs=pl.BlockSpec((TM, D_MODEL), lambda i: (i, 0)),
        out_shape=jax.ShapeDtypeStruct((n, D_MODEL), F32),
        scratch_shapes=[pltpu.VMEM((ext, D_MODEL), BF16),
                        pltpu.VMEM((2 * D_FF // LANES, ext, LANES), F32)],
        compiler_params=pltpu.CompilerParams(
            dimension_semantics=("parallel",), vmem_limit_bytes=VMEM_LIMIT),
        name="ffn",
    )(x, x, x, g, wup, cw, wdn, fg)


def _prep_layer(l, norm1_g, w_in, b_gate, conv_a_w, conv_a_b, ln_a_g, ln_a_b, w_out_a, pool_w,
                pool_scale, w_out_b, sc_w, w_out_c, rpb, w_out_d, w_o, norm2_g, w_up, ffn_conv_w,
                w_down):
    row = lambda v: v.reshape(1, -1).astype(F32)
    col_scale = (jnp.ones((w_in.shape[-1],), F32).at[OFF_D:OFF_D + D_ATT].set(HEAD_DIM ** -0.5)
                 .at[OFF_G:].set(0.5))
    up_scale = jnp.ones((w_up.shape[-1],), F32).at[D_FF:].set(0.5)
    pool_bd = jax.scipy.linalg.block_diag(*[pool_w[l, gi] for gi in range(len(POOL_WINDOWS))])
    w = (w_in[l] * col_scale).astype(BF16)
    return dict(
        norm1_g=row(norm1_g[l]),
        w_in=jnp.concatenate([w[:, :OFF_D], w[:, OFF_D + D_ATT:OFF_G], w[:, OFF_D:OFF_D + D_ATT],
                              w[:, OFF_G:]], axis=1),
        half_b_gate=row(b_gate[l]) * 0.5,
        bias=_attn_bias_table(rpb[l]),
        conv=(conv_a_w[l].astype(F32), row(conv_a_b[l]), row(ln_a_g[l]), row(ln_a_b[l]),
              sc_w[l].astype(F32)),
        mixer=(w_out_a[l].astype(BF16), pool_bd.astype(BF16), row(pool_scale[l]),
               w_out_b[l].astype(BF16), w_out_c[l].astype(BF16), w_out_d[l].astype(BF16),
               w_o[l].astype(BF16)),
        norm2_g=row(norm2_g[l]),
        w_up=(w_up[l] * up_scale).astype(BF16),
        ffn_conv_w=ffn_conv_w[l].astype(F32),
        w_down=w_down[l].astype(BF16),
    )


def _trunk(x3, layers, final_g):
    b, t, d = x3.shape
    x = x3.reshape(b * t, d)
    fg = final_g.reshape(1, -1).astype(F32)
    for li, p in enumerate(layers):
        act, zq, gate = _inproj(x, p["norm1_g"], p["w_in"], p["half_b_gate"], p["conv"], t)
        x = _mixer(zq, p["bias"], act, gate, x, p["mixer"], t)
        x = _ffn(x, p["norm2_g"], p["w_up"], p["ffn_conv_w"], p["w_down"], fg, t,
                 final=li == len(layers) - 1)
    return x.reshape(b, t, d)


def kernel(x_prompt, x_sample, norm1_g, w_in, b_gate, conv_a_w, conv_a_b, ln_a_g, ln_a_b, w_out_a,
           pool_w, pool_scale, w_out_b, sc_w, w_out_c, rpb, w_out_d, w_o, norm2_g, w_up, ffn_conv_w,
           w_down, final_g):
    depth = w_in.shape[0]
    layers = [_prep_layer(l, norm1_g, w_in, b_gate, conv_a_w, conv_a_b, ln_a_g, ln_a_b, w_out_a,
                          pool_w, pool_scale, w_out_b, sc_w, w_out_c, rpb, w_out_d, w_o, norm2_g,
                          w_up, ffn_conv_w, w_down)
              for l in range(depth)]
    return _trunk(x_prompt, layers, final_g), _trunk(x_sample, layers, final_g)
```

```python
import functools

import numpy as np
import jax
import jax.numpy as jnp
from jax import lax
from jax.experimental import pallas as pl
from jax.experimental.pallas import tpu as pltpu

F32 = jnp.float32
BF16 = jnp.bfloat16

D_MODEL = 1024
GRID_W = 64
D_CONV = 256
CONV_K = 31
D_POOL = 256
POOL_WINDOWS = (2, 4, 8, 16)
POOL_GROUP = D_POOL // len(POOL_WINDOWS)
D_SC = 256
SC_K = 3
N_HEADS = 4
HEAD_DIM = 64
D_ATT = N_HEADS * HEAD_DIM
WIN_ROWS = 8
WIN_COLS = 16
N_BRANCH = 4
D_FF = 2816
FFN_K = 3
EPS = 1e-6
NEG_INF = -1e30

OFF_B = 2 * D_CONV
OFF_C = OFF_B + D_POOL
OFF_D = OFF_C + 3 * D_SC
OFF_G = OFF_D + 3 * D_ATT
D_GATE = N_BRANCH * D_MODEL
D_ACT = D_CONV + D_POOL + D_SC

WIN_TOKENS = WIN_ROWS * GRID_W
LANES = 128
HALO = 16
GATE_CHUNKS = 16
CONV_ROWS = 64
PACE_CHUNKS = 16
FFN_HALO = 8
FF_CHUNKS = ((0, 768), (768, 1536), (1536, 2304), (2304, D_FF))

VMEM_LIMIT = 56 * 1024 * 1024
TM = 512


def _const_spec(shape):
    nd = len(shape)
    return pl.BlockSpec(shape, lambda *_: (0,) * nd, pipeline_mode=pl.Buffered(1))


def _rms(x, g):
    ms = jnp.mean(x * x, axis=-1, keepdims=True)
    return x * lax.rsqrt(ms + EPS) * g


def _half_sigmoid(half_x):
    return 0.5 * jnp.tanh(half_x) + 0.5


def _inproj_kernel(xc_ref, xp_ref, xn_ref, g_ref, w_ref, hbg_ref,
                   caw_ref, cab_ref, lng_ref, lnb_ref, scw_ref,
                   act_ref, zq_ref, gate_ref, a_scr, b_scr, c_scr, p_scr, *, t):
    tiles_per_seq = t // TM
    j = pl.program_id(0) % tiles_per_seq
    keep_p = jnp.where(j == 0, 0.0, 1.0)
    keep_n = jnp.where(j == tiles_per_seq - 1, 0.0, 1.0)
    ext = TM + 2 * HALO
    g = g_ref[...]
    h = _rms(xc_ref[...], g).astype(BF16)
    h_ext = jnp.concatenate([_rms(xp_ref[...], g).astype(BF16), h, _rms(xn_ref[...], g).astype(BF16)],
                            axis=0)

    zbc = jnp.dot(h_ext, w_ref[:, OFF_B:OFF_D], preferred_element_type=F32)
    za = jnp.dot(h_ext, w_ref[:, :OFF_B], preferred_element_type=F32)
    row = lax.broadcasted_iota(jnp.int32, (ext, 1), 0)
    keep = jnp.where(row < HALO, keep_p, jnp.where(row >= HALO + TM, keep_n, 1.0))
    staged = ((a_scr, za[:, 0:D_CONV] * jax.nn.sigmoid(za[:, D_CONV:OFF_B])),
              (b_scr, zbc[:, 0:D_POOL]),
              (c_scr, zbc[:, D_POOL + D_SC:D_POOL + 2 * D_SC] * zbc[:, D_POOL + 2 * D_SC:]))
    for scr, val in staged:
        val = val * keep
        for s in range(val.shape[1] // LANES):
            scr[s] = val[:, s * LANES:(s + 1) * LANES]
    c_gate = zbc[HALO:HALO + TM, D_POOL:D_POOL + D_SC]

    never = pl.program_id(0) < 0
    pace = []
    cw = D_GATE // GATE_CHUNKS
    for ci in range(GATE_CHUNKS):
        cols = slice(ci * cw, (ci + 1) * cw)
        zg = jnp.dot(h, w_ref[:, OFF_G + ci * cw:OFF_G + (ci + 1) * cw], preferred_element_type=F32)
        gate_ref[:, cols] = _half_sigmoid(zg + hbg_ref[:, cols]).astype(BF16)
        pace.append(zg[TM - 1:TM, cw - LANES:cw])

    def paced(step, n_steps, v):
        return jnp.where(never, pace[step * PACE_CHUNKS // n_steps], v)

    def rows(lo, hi):
        return slice(HALO + lo, HALO + TM + hi)

    def depthwise(scr, wt_ref, taps, r0, nr, pacing):
        cols = []
        for s in range(scr.shape[0]):
            out = None
            for k in range(taps):
                start = HALO + r0 + k - taps // 2
                wk = wt_ref[k:k + 1, s * LANES:(s + 1) * LANES]
                if pacing is not None:
                    wk = paced(pacing[0] + s * taps + k, pacing[1], wk)
                term = wk * scr[s, start:start + nr, :]
                out = term if out is None else out + term
            cols.append(out)
        return jnp.concatenate(cols, axis=1)

    n_blk = CONV_K * (D_CONV // LANES)
    n_steps = (TM // CONV_ROWS) * n_blk
    for rb in range(TM // CONV_ROWS):
        r0 = rb * CONV_ROWS
        acc = depthwise(a_scr, caw_ref, CONV_K, r0, CONV_ROWS, (rb * n_blk, n_steps)) + cab_ref[...]
        mu = jnp.mean(acc, axis=-1, keepdims=True)
        d = acc - mu
        var = jnp.mean(d * d, axis=-1, keepdims=True)
        a = d * lax.rsqrt(var + EPS) * lng_ref[...] + lnb_ref[...]
        act_ref[r0:r0 + CONV_ROWS, 0:D_CONV] = (a * jax.nn.sigmoid(a)).astype(BF16)

    lane = lax.broadcasted_iota(jnp.int32, (TM, LANES), 1)
    low_half = lane < POOL_GROUP
    pos = j * TM + lax.broadcasted_iota(jnp.int32, (TM, LANES), 0)
    for s in range(D_POOL // LANES):
        s2_scr, s4_scr = p_scr.at[0, s], p_scr.at[1, s]
        s8_scr = s2_scr
        s2_scr[rows(-7, 7), :] = b_scr[s, rows(-8, 6), :] + b_scr[s, rows(-7, 7), :]
        s4_scr[rows(-6, 6), :] = s2_scr[rows(-7, 5), :] + s2_scr[rows(-5, 7), :]
        if s == 0:
            wsum = jnp.where(low_half, s2_scr[rows(0, 0), :], s4_scr[rows(0, 0), :])
        else:
            s8_scr[rows(-4, 4), :] = s4_scr[rows(-6, 2), :] + s4_scr[rows(-2, 6), :]
            s16 = s8_scr[rows(-4, -4), :] + s8_scr[rows(4, 4), :]
            wsum = jnp.where(low_half, s8_scr[rows(0, 0), :], s16)
        half = jnp.where(low_half, POOL_WINDOWS[2 * s] // 2, POOL_WINDOWS[2 * s + 1] // 2)
        cnt = (jnp.minimum(pos + half - 1, t - 1) - jnp.maximum(pos - half, 0) + 1).astype(F32)
        act_ref[:, D_CONV + s * LANES:D_CONV + (s + 1) * LANES] = (
            wsum / cnt - b_scr[s, rows(0, 0), :]).astype(BF16)

    act_ref[:, D_CONV + D_POOL:D_ACT] = (
        c_gate * depthwise(c_scr, scw_ref, SC_K, 0, TM, None)).astype(BF16)
    zq_ref[...] = jnp.dot(h, w_ref[:, OFF_D:OFF_G], preferred_element_type=F32).astype(BF16)


def _inproj(x, g, w, hbg, conv_wts, t):
    n = x.shape[0]
    hb = TM // HALO
    nhb = n // HALO
    ext = TM + 2 * HALO
    tok = lambda width: pl.BlockSpec((TM, width), lambda i: (i, 0))
    return pl.pallas_call(
        functools.partial(_inproj_kernel, t=t),
        grid=(n // TM,),
        in_specs=[tok(D_MODEL),
                  pl.BlockSpec((HALO, D_MODEL), lambda i: (jnp.maximum(i * hb - 1, 0), 0)),
                  pl.BlockSpec((HALO, D_MODEL), lambda i: (jnp.minimum((i + 1) * hb, nhb - 1), 0)),
                  _const_spec((1, D_MODEL)),
                  _const_spec(w.shape),
                  _const_spec(hbg.shape)] + [_const_spec(c.shape) for c in conv_wts],
        out_specs=[tok(D_ACT), tok(3 * D_ATT), tok(D_GATE)],
        out_shape=[jax.ShapeDtypeStruct((n, D_ACT), BF16),
                   jax.ShapeDtypeStruct((n, 3 * D_ATT), BF16),
                   jax.ShapeDtypeStruct((n, D_GATE), BF16)],
        scratch_shapes=[pltpu.VMEM((D_CONV // LANES, ext, LANES), F32),
                        pltpu.VMEM((D_POOL // LANES, ext, LANES), F32),
                        pltpu.VMEM((D_SC // LANES, ext, LANES), F32),
                        pltpu.VMEM((2, D_POOL // LANES, ext, LANES), F32)],
        compiler_params=pltpu.CompilerParams(
            dimension_semantics=("parallel",), vmem_limit_bytes=VMEM_LIMIT),
        name="inproj",
    )(x, x, x, g, w, hbg, *conv_wts)


def _attention_rows(q_ref, kv_scr, bias_ref, att_scr, first_row, rows):
    lane = lax.broadcasted_iota(jnp.int32, (1, D_ATT), 1)
    head_masks = [lane // HEAD_DIM == h for h in range(N_HEADS)]
    for r in range(TM // GRID_W):
        row = first_row + r
        rs = jnp.clip(row - WIN_ROWS // 2, 0, rows - WIN_ROWS)
        k0 = pl.multiple_of((rs - first_row) * GRID_W + TM, GRID_W)
        q = q_ref[r * GRID_W:(r + 1) * GRID_W, :]
        k = kv_scr[pl.ds(k0, WIN_TOKENS), 0:D_ATT]
        v = kv_scr[pl.ds(k0, WIN_TOKENS), D_ATT:2 * D_ATT]
        zero = jnp.zeros_like(q)
        qm = jnp.concatenate([jnp.where(m, q, zero) for m in head_masks], axis=0)
        s = lax.dot_general(qm, k, (((1,), (1,)), ((), ())), preferred_element_type=F32)
        s = s + bias_ref[row - rs]
        m = jnp.max(s, axis=-1, keepdims=True)
        p = jnp.exp(s - m)
        l = jnp.sum(p, axis=-1, keepdims=True)
        o = jnp.dot(p.astype(BF16), v, preferred_element_type=F32)
        o = o * (1.0 / l)
        out = jnp.zeros((GRID_W, D_ATT), F32)
        for h, hm in enumerate(head_masks):
            out = jnp.where(hm, o[h * GRID_W:(h + 1) * GRID_W], out)
        att_scr[r * GRID_W:(r + 1) * GRID_W, :] = out.astype(BF16)


def _attn_bias_table(rpb):
    qcol = np.arange(GRID_W)[:, None]
    kcol = np.arange(GRID_W)[None, :]
    cstart = np.clip(qcol - WIN_COLS // 2, 0, GRID_W - WIN_COLS)
    col_mask = (kcol >= cstart) & (kcol < cstart + WIN_COLS)
    col_off = np.clip(kcol - qcol + WIN_COLS - 1, 0, 2 * WIN_COLS - 2)
    b = jnp.where(jnp.asarray(col_mask), rpb.astype(F32)[..., col_off], NEG_INF)
    b = jnp.stack([b[:, WIN_ROWS - 1 - s:2 * WIN_ROWS - 1 - s] for s in range(WIN_ROWS)])
    return b.transpose(0, 1, 3, 2, 4).reshape(WIN_ROWS, N_HEADS * GRID_W, WIN_TOKENS)


def _mixer_kernel(q_ref, kvp_ref, kvc_ref, kvn_ref, bias_ref, act_ref, gate_ref, x_ref,
                  woa_ref, pw_ref, ps_ref, wob_ref, woc_ref, wod_ref, wo_ref, o_ref,
                  kv_scr, att_scr, *, t):
    kv_scr[0:TM, :] = kvp_ref[...]
    kv_scr[TM:2 * TM, :] = kvc_ref[...]
    kv_scr[2 * TM:3 * TM, :] = kvn_ref[...]
    first_row = (pl.program_id(0) % (t // TM)) * (TM // GRID_W)
    _attention_rows(q_ref, kv_scr, bias_ref, att_scr, first_row, t // GRID_W)

    def proj(v, w_ref):
        return jnp.dot(v, w_ref[...], preferred_element_type=F32)

    pool = proj(act_ref[:, D_CONV:D_CONV + D_POOL], pw_ref) * ps_ref[...]
    branches = (proj(act_ref[:, 0:D_CONV], woa_ref),
                proj(pool.astype(BF16), wob_ref),
                proj(act_ref[:, D_CONV + D_POOL:D_ACT], woc_ref),
                proj(att_scr[...], wod_ref))
    merged = None
    for bi, br in enumerate(branches):
        term = gate_ref[:, bi * D_MODEL:(bi + 1) * D_MODEL].astype(F32) * br
        merged = term if merged is None else merged + term
    o_ref[...] = x_ref[...] + proj(merged.astype(BF16), wo_ref)


def _mixer(zq, bias, act, gate, x, wts, t):
    n = x.shape[0]
    nt = n // TM
    tok = lambda width, col=0: pl.BlockSpec((TM, width), lambda i: (i, col))
    kv = lambda index: pl.BlockSpec((TM, 2 * D_ATT), lambda i: (index(i), 0))
    return pl.pallas_call(
        functools.partial(_mixer_kernel, t=t),
        grid=(nt,),
        in_specs=[tok(D_ATT, 2),
                  kv(lambda i: jnp.maximum(i - 1, 0)), kv(lambda i: i),
                  kv(lambda i: jnp.minimum(i + 1, nt - 1)),
                  _const_spec(bias.shape),
                  tok(D_ACT), tok(D_GATE), tok(D_MODEL)]
        + [_const_spec(w.shape) for w in wts],
        out_specs=tok(D_MODEL),
        out_shape=jax.ShapeDtypeStruct((n, D_MODEL), F32),
        scratch_shapes=[pltpu.VMEM((3 * TM, 2 * D_ATT), BF16),
                        pltpu.VMEM((TM, D_ATT), BF16)],
        compiler_params=pltpu.CompilerParams(
            dimension_semantics=("parallel",), vmem_limit_bytes=VMEM_LIMIT),
        name="mixer",
    )(zq, zq, zq, zq, bias, act, gate, x, *wts)


def _ffn_kernel(xc_ref, xp_ref, xn_ref, g_ref, wup_ref, cw_ref, wdn_ref, fg_ref, o_ref,
                u_scr, *, t, final):
    tiles_per_seq = t // TM
    j = pl.program_id(0) % tiles_per_seq
    keep_p = jnp.where(j == 0, 0.0, 1.0)
    keep_n = jnp.where(j == tiles_per_seq - 1, 0.0, 1.0)
    g = g_ref[...]
    x = xc_ref[...]
    h = jnp.concatenate([_rms(xp_ref[...], g) * keep_p, _rms(x, g), _rms(xn_ref[...], g) * keep_n],
                        axis=0).astype(BF16)

    def up(c0, c1):
        for off in (0, D_FF):
            u = jnp.dot(h, wup_ref[:, off + c0:off + c1], preferred_element_type=F32)
            for s in range((c1 - c0) // LANES):
                u_scr[(off + c0) // LANES + s] = u[:, s * LANES:(s + 1) * LANES]

    def conv(slab):
        out = None
        for k in range(FFN_K):
            r0 = FFN_HALO - FFN_K // 2 + k
            term = cw_ref[k:k + 1, slab * LANES:(slab + 1) * LANES] * u_scr[slab, r0:r0 + TM, :]
            out = term if out is None else out + term
        return out

    def glu(c0, c1):
        cols = []
        for s in range(c0 // LANES, c1 // LANES):
            val = conv(s)
            hg = conv(D_FF // LANES + s)
            cols.append((val * (hg * (1.0 + jnp.tanh(hg)))).astype(BF16))
        return jnp.concatenate(cols, axis=1)

    acc = x
    up(*FF_CHUNKS[0])
    for ci, (c0, c1) in enumerate(FF_CHUNKS):
        if ci + 1 < len(FF_CHUNKS):
            up(*FF_CHUNKS[ci + 1])
        acc = acc + jnp.dot(glu(c0, c1), wdn_ref[c0:c1, :], preferred_element_type=F32)
    if final:
        acc = _rms(acc, fg_ref[...])
    o_ref[...] = acc


def _ffn(x, g, wup, cw, wdn, fg, t, final):
    n = x.shape[0]
    hb = TM // FFN_HALO
    nhb = n // FFN_HALO
    ext = TM + 2 * FFN_HALO
    return pl.pallas_call(
        functools.partial(_ffn_kernel, t=t, final=final),
        grid=(n // TM,),
        in_specs=[pl.BlockSpec((TM, D_MODEL), lambda i: (i, 0)),
                  pl.BlockSpec((FFN_HALO, D_MODEL), lambda i: (jnp.maximum(i * hb - 1, 0), 0)),
                  pl.BlockSpec((FFN_HALO, D_MODEL), lambda i: (jnp.minimum((i + 1) * hb, nhb - 1), 0)),
                  _const_spec(g.shape), _const_spec(wup.shape), _const_spec(cw.shape),
                  _const_spec(wdn.shape), _const_spec(fg.shape)],
        out_specs=pl.BlockSpec((TM, D_MODEL), lambda i: (i, 0)),
        out_shape=jax.ShapeDtypeStruct((n, D_MODEL), F32),
        scratch_shapes=[pltpu.VMEM((2 * D_FF // LANES, ext, LANES), F32)],
        compiler_params=pltpu.CompilerParams(
            dimension_semantics=("parallel",), vmem_limit_bytes=VMEM_LIMIT),
        name="ffn",
    )(x, x, x, g, wup, cw, wdn, fg)


def _prep_layer(l, norm1_g, w_in, b_gate, conv_a_w, conv_a_b, ln_a_g, ln_a_b, w_out_a, pool_w,
                pool_scale, w_out_b, sc_w, w_out_c, rpb, w_out_d, w_o, norm2_g, w_up, ffn_conv_w,
                w_down):
    row = lambda v: v.reshape(1, -1).astype(F32)
    col_scale = (jnp.ones((w_in.shape[-1],), F32).at[OFF_D:OFF_D + D_ATT].set(HEAD_DIM ** -0.5)
                 .at[OFF_G:].set(0.5))
    up_scale = jnp.ones((w_up.shape[-1],), F32).at[D_FF:].set(0.5)
    pool_bd = jax.scipy.linalg.block_diag(*[pool_w[l, gi] for gi in range(len(POOL_WINDOWS))])
    w = (w_in[l] * col_scale).astype(BF16)
    return dict(
        norm1_g=row(norm1_g[l]),
        w_in=jnp.concatenate([w[:, :OFF_D], w[:, OFF_D + D_ATT:OFF_G], w[:, OFF_D:OFF_D + D_ATT],
                              w[:, OFF_G:]], axis=1),
        half_b_gate=row(b_gate[l]) * 0.5,
        bias=_attn_bias_table(rpb[l]),
        conv=(conv_a_w[l].astype(F32), row(conv_a_b[l]), row(ln_a_g[l]), row(ln_a_b[l]),
              sc_w[l].astype(F32)),
        mixer=(w_out_a[l].astype(BF16), pool_bd.astype(BF16), row(pool_scale[l]),
               w_out_b[l].astype(BF16), w_out_c[l].astype(BF16), w_out_d[l].astype(BF16),
               w_o[l].astype(BF16)),
        norm2_g=row(norm2_g[l]),
        w_up=(w_up[l] * up_scale).astype(BF16),
        ffn_conv_w=ffn_conv_w[l].astype(F32),
        w_down=w_down[l].astype(BF16),
    )


def _trunk(x3, layers, final_g):
    b, t, d = x3.shape
    x = x3.reshape(b * t, d)
    fg = final_g.reshape(1, -1).astype(F32)
    for li, p in enumerate(layers):
        act, zq, gate = _inproj(x, p["norm1_g"], p["w_in"], p["half_b_gate"], p["conv"], t)
        x = _mixer(zq, p["bias"], act, gate, x, p["mixer"], t)
        x = _ffn(x, p["norm2_g"], p["w_up"], p["ffn_conv_w"], p["w_down"], fg, t,
                 final=li == len(layers) - 1)
    return x.reshape(b, t, d)


def kernel(x_prompt, x_sample, norm1_g, w_in, b_gate, conv_a_w, conv_a_b, ln_a_g, ln_a_b, w_out_a,
           pool_w, pool_scale, w_out_b, sc_w, w_out_c, rpb, w_out_d, w_o, norm2_g, w_up, ffn_conv_w,
           w_down, final_g):
    depth = w_in.shape[0]
    layers = [_prep_layer(l, norm1_g, w_in, b_gate, conv_a_w, conv_a_b, ln_a_g, ln_a_b, w_out_a,
                          pool_w, pool_scale, w_out_b, sc_w, w_out_c, rpb, w_out_d, w_o, norm2_g,
                          w_up, ffn_conv_w, w_down)
              for l in range(depth)]
    return _trunk(x_prompt, layers, final_g), _trunk(x_sample, layers, final_g)
```

```python
import functools

import numpy as np
import jax
import jax.numpy as jnp
from jax import lax
from jax.experimental import pallas as pl
from jax.experimental.pallas import tpu as pltpu

F32 = jnp.float32
BF16 = jnp.bfloat16

D_MODEL = 1024
GRID_W = 64
D_CONV = 256
CONV_K = 31
D_POOL = 256
POOL_WINDOWS = (2, 4, 8, 16)
POOL_GROUP = D_POOL // len(POOL_WINDOWS)
D_SC = 256
SC_K = 3
N_HEADS = 4
HEAD_DIM = 64
D_ATT = N_HEADS * HEAD_DIM
WIN_ROWS = 8
WIN_COLS = 16
N_BRANCH = 4
D_FF = 2816
FFN_K = 3
EPS = 1e-6
NEG_INF = -1e30
LOG2E = float(np.log2(np.e))

OFF_B = 2 * D_CONV
OFF_C = OFF_B + D_POOL
OFF_D = OFF_C + 3 * D_SC
OFF_G = OFF_D + 3 * D_ATT
D_GATE = N_BRANCH * D_MODEL
D_ACT = D_CONV + D_POOL + D_SC

WIN_TOKENS = WIN_ROWS * GRID_W
LANES = 128
HALO = 16
GATE_CHUNKS = 16
CONV_ROWS = 64
PACE_CHUNKS = 16
FFN_HALO = 8
FF_CHUNKS = ((0, 768), (768, 1536), (1536, 2304), (2304, D_FF))

VMEM_LIMIT = 56 * 1024 * 1024
TM = 512


def _const_spec(shape):
    nd = len(shape)
    return pl.BlockSpec(shape, lambda *_: (0,) * nd, pipeline_mode=pl.Buffered(1))


def _rms(x, g):
    ms = jnp.mean(x * x, axis=-1, keepdims=True)
    return x * lax.rsqrt(ms + EPS) * g


def _half_sigmoid(half_x):
    return 0.5 * jnp.tanh(half_x) + 0.5


def _inproj_kernel(xc_ref, xp_ref, xn_ref, g_ref, w_ref, hbg_ref,
                   caw_ref, cab_ref, lng_ref, lnb_ref, scw_ref,
                   act_ref, zq_ref, gate_ref, a_scr, b_scr, c_scr, p_scr, *, t):
    tiles_per_seq = t // TM
    j = pl.program_id(0) % tiles_per_seq
    keep_p = jnp.where(j == 0, 0.0, 1.0)
    keep_n = jnp.where(j == tiles_per_seq - 1, 0.0, 1.0)
    ext = TM + 2 * HALO
    g = g_ref[...]
    h = _rms(xc_ref[...], g).astype(BF16)
    h_ext = jnp.concatenate([_rms(xp_ref[...], g).astype(BF16), h, _rms(xn_ref[...], g).astype(BF16)],
                            axis=0)

    zbc = jnp.dot(h_ext, w_ref[:, OFF_B:OFF_D], preferred_element_type=F32)
    za = jnp.dot(h_ext, w_ref[:, :OFF_B], preferred_element_type=F32)
    row = lax.broadcasted_iota(jnp.int32, (ext, 1), 0)
    keep = jnp.where(row < HALO, keep_p, jnp.where(row >= HALO + TM, keep_n, 1.0))
    staged = ((a_scr, za[:, 0:D_CONV] * jax.nn.sigmoid(za[:, D_CONV:OFF_B])),
              (b_scr, zbc[:, 0:D_POOL]),
              (c_scr, zbc[:, D_POOL + D_SC:D_POOL + 2 * D_SC] * zbc[:, D_POOL + 2 * D_SC:]))
    for scr, val in staged:
        val = val * keep
        for s in range(val.shape[1] // LANES):
            scr[s] = val[:, s * LANES:(s + 1) * LANES]
    c_gate = zbc[HALO:HALO + TM, D_POOL:D_POOL + D_SC]

    never = pl.program_id(0) < 0
    pace = []
    cw = D_GATE // GATE_CHUNKS
    for ci in range(GATE_CHUNKS):
        cols = slice(ci * cw, (ci + 1) * cw)
        zg = jnp.dot(h, w_ref[:, OFF_G + ci * cw:OFF_G + (ci + 1) * cw], preferred_element_type=F32)
        gate_ref[:, cols] = _half_sigmoid(zg + hbg_ref[:, cols]).astype(BF16)
        pace.append(zg[TM - 1:TM, cw - LANES:cw])

    def paced(step, n_steps, v):
        return jnp.where(never, pace[step * PACE_CHUNKS // n_steps], v)

    def rows(lo, hi):
        return slice(HALO + lo, HALO + TM + hi)

    def depthwise(scr, wt_ref, taps, r0, nr, pacing):
        cols = []
        for s in range(scr.shape[0]):
            out = None
            for k in range(taps):
                start = HALO + r0 + k - taps // 2
                wk = wt_ref[k:k + 1, s * LANES:(s + 1) * LANES]
                if pacing is not None:
                    wk = paced(pacing[0] + s * taps + k, pacing[1], wk)
                term = wk * scr[s, start:start + nr, :]
                out = term if out is None else out + term
            cols.append(out)
        return jnp.concatenate(cols, axis=1)

    n_blk = CONV_K * (D_CONV // LANES)
    n_steps = (TM // CONV_ROWS) * n_blk
    for rb in range(TM // CONV_ROWS):
        r0 = rb * CONV_ROWS
        acc = depthwise(a_scr, caw_ref, CONV_K, r0, CONV_ROWS, (rb * n_blk, n_steps)) + cab_ref[...]
        mu = jnp.mean(acc, axis=-1, keepdims=True)
        d = acc - mu
        var = jnp.mean(d * d, axis=-1, keepdims=True)
        a = d * lax.rsqrt(var + EPS) * lng_ref[...] + lnb_ref[...]
        act_ref[r0:r0 + CONV_ROWS, 0:D_CONV] = (a * jax.nn.sigmoid(a)).astype(BF16)

    lane = lax.broadcasted_iota(jnp.int32, (TM, LANES), 1)
    low_half = lane < POOL_GROUP
    pos = j * TM + lax.broadcasted_iota(jnp.int32, (TM, LANES), 0)
    for s in range(D_POOL // LANES):
        s2_scr, s4_scr = p_scr.at[0, s], p_scr.at[1, s]
        s8_scr = s2_scr
        s2_scr[rows(-7, 7), :] = b_scr[s, rows(-8, 6), :] + b_scr[s, rows(-7, 7), :]
        s4_scr[rows(-6, 6), :] = s2_scr[rows(-7, 5), :] + s2_scr[rows(-5, 7), :]
        if s == 0:
            wsum = jnp.where(low_half, s2_scr[rows(0, 0), :], s4_scr[rows(0, 0), :])
        else:
            s8_scr[rows(-4, 4), :] = s4_scr[rows(-6, 2), :] + s4_scr[rows(-2, 6), :]
            s16 = s8_scr[rows(-4, -4), :] + s8_scr[rows(4, 4), :]
            wsum = jnp.where(low_half, s8_scr[rows(0, 0), :], s16)
        half = jnp.where(low_half, POOL_WINDOWS[2 * s] // 2, POOL_WINDOWS[2 * s + 1] // 2)
        cnt = (jnp.minimum(pos + half - 1, t - 1) - jnp.maximum(pos - half, 0) + 1).astype(F32)
        act_ref[:, D_CONV + s * LANES:D_CONV + (s + 1) * LANES] = (
            wsum / cnt - b_scr[s, rows(0, 0), :]).astype(BF16)

    act_ref[:, D_CONV + D_POOL:D_ACT] = (
        c_gate * depthwise(c_scr, scw_ref, SC_K, 0, TM, None)).astype(BF16)
    zq_ref[...] = jnp.dot(h, w_ref[:, OFF_D:OFF_G], preferred_element_type=F32).astype(BF16)


def _inproj(x, g, w, hbg, conv_wts, t):
    n = x.shape[0]
    hb = TM // HALO
    nhb = n // HALO
    ext = TM + 2 * HALO
    tok = lambda width: pl.BlockSpec((TM, width), lambda i: (i, 0))
    return pl.pallas_call(
        functools.partial(_inproj_kernel, t=t),
        grid=(n // TM,),
        in_specs=[tok(D_MODEL),
                  pl.BlockSpec((HALO, D_MODEL), lambda i: (jnp.maximum(i * hb - 1, 0), 0)),
                  pl.BlockSpec((HALO, D_MODEL), lambda i: (jnp.minimum((i + 1) * hb, nhb - 1), 0)),
                  _const_spec((1, D_MODEL)),
                  _const_spec(w.shape),
                  _const_spec(hbg.shape)] + [_const_spec(c.shape) for c in conv_wts],
        out_specs=[tok(D_ACT), tok(3 * D_ATT), tok(D_GATE)],
        out_shape=[jax.ShapeDtypeStruct((n, D_ACT), BF16),
                   jax.ShapeDtypeStruct((n, 3 * D_ATT), BF16),
                   jax.ShapeDtypeStruct((n, D_GATE), BF16)],
        scratch_shapes=[pltpu.VMEM((D_CONV // LANES, ext, LANES), F32),
                        pltpu.VMEM((D_POOL // LANES, ext, LANES), F32),
                        pltpu.VMEM((D_SC // LANES, ext, LANES), F32),
                        pltpu.VMEM((2, D_POOL // LANES, ext, LANES), F32)],
        compiler_params=pltpu.CompilerParams(
            dimension_semantics=("parallel",), vmem_limit_bytes=VMEM_LIMIT),
        name="inproj",
    )(x, x, x, g, w, hbg, *conv_wts)


def _attention_rows(q_ref, kv_scr, bias_ref, att_scr, first_row, rows):
    lane = lax.broadcasted_iota(jnp.int32, (1, D_ATT), 1)
    head_masks = [lane // HEAD_DIM == h for h in range(N_HEADS)]
    for r in range(TM // GRID_W):
        row = first_row + r
        rs = jnp.clip(row - WIN_ROWS // 2, 0, rows - WIN_ROWS)
        k0 = pl.multiple_of((rs - first_row) * GRID_W + TM, GRID_W)
        q = q_ref[r * GRID_W:(r + 1) * GRID_W, :]
        k = kv_scr[pl.ds(k0, WIN_TOKENS), 0:D_ATT]
        v = kv_scr[pl.ds(k0, WIN_TOKENS), D_ATT:2 * D_ATT]
        zero = jnp.zeros_like(q)
        qm = jnp.concatenate([jnp.where(m, q, zero) for m in head_masks], axis=0)
        s = lax.dot_general(qm, k, (((1,), (1,)), ((), ())), preferred_element_type=F32)
        s = s + bias_ref[row - rs]
        m = jnp.max(s, axis=-1, keepdims=True)
        p = jnp.exp2(s - m)
        l = jnp.sum(p, axis=-1, keepdims=True)
        o = jnp.dot(p.astype(BF16), v, preferred_element_type=F32)
        o = o * (1.0 / l)
        out = jnp.zeros((GRID_W, D_ATT), F32)
        for h, hm in enumerate(head_masks):
            out = jnp.where(hm, o[h * GRID_W:(h + 1) * GRID_W], out)
        att_scr[r * GRID_W:(r + 1) * GRID_W, :] = out.astype(BF16)


def _attn_bias_table(rpb):
    qcol = np.arange(GRID_W)[:, None]
    kcol = np.arange(GRID_W)[None, :]
    cstart = np.clip(qcol - WIN_COLS // 2, 0, GRID_W - WIN_COLS)
    col_mask = (kcol >= cstart) & (kcol < cstart + WIN_COLS)
    col_off = np.clip(kcol - qcol + WIN_COLS - 1, 0, 2 * WIN_COLS - 2)
    b = jnp.where(jnp.asarray(col_mask), rpb.astype(F32)[..., col_off] * LOG2E, NEG_INF)
    b = jnp.stack([b[:, WIN_ROWS - 1 - s:2 * WIN_ROWS - 1 - s] for s in range(WIN_ROWS)])
    return b.transpose(0, 1, 3, 2, 4).reshape(WIN_ROWS, N_HEADS * GRID_W, WIN_TOKENS)


def _mixer_kernel(q_ref, kvp_ref, kvc_ref, kvn_ref, bias_ref, act_ref, gate_ref, x_ref,
                  woa_ref, pw_ref, ps_ref, wob_ref, woc_ref, wod_ref, wo_ref, o_ref,
                  kv_scr, att_scr, *, t):
    kv_scr[0:TM, :] = kvp_ref[...]
    kv_scr[TM:2 * TM, :] = kvc_ref[...]
    kv_scr[2 * TM:3 * TM, :] = kvn_ref[...]
    first_row = (pl.program_id(0) % (t // TM)) * (TM // GRID_W)
    _attention_rows(q_ref, kv_scr, bias_ref, att_scr, first_row, t // GRID_W)

    def proj(v, w_ref):
        return jnp.dot(v, w_ref[...], preferred_element_type=F32)

    pool = proj(act_ref[:, D_CONV:D_CONV + D_POOL], pw_ref) * ps_ref[...]
    branches = (proj(act_ref[:, 0:D_CONV], woa_ref),
                proj(pool.astype(BF16), wob_ref),
                proj(act_ref[:, D_CONV + D_POOL:D_ACT], woc_ref),
                proj(att_scr[...], wod_ref))
    merged = None
    for bi, br in enumerate(branches):
        term = gate_ref[:, bi * D_MODEL:(bi + 1) * D_MODEL].astype(F32) * br
        merged = term if merged is None else merged + term
    o_ref[...] = x_ref[...] + proj(merged.astype(BF16), wo_ref)


def _mixer(zq, bias, act, gate, x, wts, t):
    n = x.shape[0]
    nt = n // TM
    tok = lambda width, col=0: pl.BlockSpec((TM, width), lambda i: (i, col))
    kv = lambda index: pl.BlockSpec((TM, 2 * D_ATT), lambda i: (index(i), 0))
    return pl.pallas_call(
        functools.partial(_mixer_kernel, t=t),
        grid=(nt,),
        in_specs=[tok(D_ATT, 2),
                  kv(lambda i: jnp.maximum(i - 1, 0)), kv(lambda i: i),
                  kv(lambda i: jnp.minimum(i + 1, nt - 1)),
                  _const_spec(bias.shape),
                  tok(D_ACT), tok(D_GATE), tok(D_MODEL)]
        + [_const_spec(w.shape) for w in wts],
        out_specs=tok(D_MODEL),
        out_shape=jax.ShapeDtypeStruct((n, D_MODEL), F32),
        scratch_shapes=[pltpu.VMEM((3 * TM, 2 * D_ATT), BF16),
                        pltpu.VMEM((TM, D_ATT), BF16)],
        compiler_params=pltpu.CompilerParams(
            dimension_semantics=("parallel",), vmem_limit_bytes=VMEM_LIMIT),
        name="mixer",
    )(zq, zq, zq, zq, bias, act, gate, x, *wts)


def _ffn_kernel(xc_ref, xp_ref, xn_ref, g_ref, wup_ref, cw_ref, wdn_ref, fg_ref, o_ref,
                u_scr, *, t, final):
    tiles_per_seq = t // TM
    j = pl.program_id(0) % tiles_per_seq
    keep_p = jnp.where(j == 0, 0.0, 1.0)
    keep_n = jnp.where(j == tiles_per_seq - 1, 0.0, 1.0)
    g = g_ref[...]
    x = xc_ref[...]
    h = jnp.concatenate([_rms(xp_ref[...], g) * keep_p, _rms(x, g), _rms(xn_ref[...], g) * keep_n],
                        axis=0).astype(BF16)

    def up(c0, c1):
        for off in (0, D_FF):
            u = jnp.dot(h, wup_ref[:, off + c0:off + c1], preferred_element_type=F32)
            for s in range((c1 - c0) // LANES):
                u_scr[(off + c0) // LANES + s] = u[:, s * LANES:(s + 1) * LANES]

    def conv(slab):
        out = None
        for k in range(FFN_K):
            r0 = FFN_HALO - FFN_K // 2 + k
            term = cw_ref[k:k + 1, slab * LANES:(slab + 1) * LANES] * u_scr[slab, r0:r0 + TM, :]
            out = term if out is None else out + term
        return out

    def glu(c0, c1):
        cols = []
        for s in range(c0 // LANES, c1 // LANES):
            val = conv(s)
            hg = conv(D_FF // LANES + s)
            cols.append((val * (hg * (1.0 + jnp.tanh(hg)))).astype(BF16))
        return jnp.concatenate(cols, axis=1)

    acc = x
    up(*FF_CHUNKS[0])
    for ci, (c0, c1) in enumerate(FF_CHUNKS):
        if ci + 1 < len(FF_CHUNKS):
            up(*FF_CHUNKS[ci + 1])
        acc = acc + jnp.dot(glu(c0, c1), wdn_ref[c0:c1, :], preferred_element_type=F32)
    if final:
        acc = _rms(acc, fg_ref[...])
    o_ref[...] = acc


def _ffn(x, g, wup, cw, wdn, fg, t, final):
    n = x.shape[0]
    hb = TM // FFN_HALO
    nhb = n // FFN_HALO
    ext = TM + 2 * FFN_HALO
    return pl.pallas_call(
        functools.partial(_ffn_kernel, t=t, final=final),
        grid=(n // TM,),
        in_specs=[pl.BlockSpec((TM, D_MODEL), lambda i: (i, 0)),
                  pl.BlockSpec((FFN_HALO, D_MODEL), lambda i: (jnp.maximum(i * hb - 1, 0), 0)),
                  pl.BlockSpec((FFN_HALO, D_MODEL), lambda i: (jnp.minimum((i + 1) * hb, nhb - 1), 0)),
                  _const_spec(g.shape), _const_spec(wup.shape), _const_spec(cw.shape),
                  _const_spec(wdn.shape), _const_spec(fg.shape)],
        out_specs=pl.BlockSpec((TM, D_MODEL), lambda i: (i, 0)),
        out_shape=jax.ShapeDtypeStruct((n, D_MODEL), F32),
        scratch_shapes=[pltpu.VMEM((2 * D_FF // LANES, ext, LANES), F32)],
        compiler_params=pltpu.CompilerParams(
            dimension_semantics=("parallel",), vmem_limit_bytes=VMEM_LIMIT),
        name="ffn",
    )(x, x, x, g, wup, cw, wdn, fg)


def _prep_layer(l, norm1_g, w_in, b_gate, conv_a_w, conv_a_b, ln_a_g, ln_a_b, w_out_a, pool_w,
                pool_scale, w_out_b, sc_w, w_out_c, rpb, w_out_d, w_o, norm2_g, w_up, ffn_conv_w,
                w_down):
    row = lambda v: v.reshape(1, -1).astype(F32)
    col_scale = (jnp.ones((w_in.shape[-1],), F32).at[OFF_D:OFF_D + D_ATT].set(HEAD_DIM ** -0.5 * LOG2E)
                 .at[OFF_G:].set(0.5))
    up_scale = jnp.ones((w_up.shape[-1],), F32).at[D_FF:].set(0.5)
    pool_bd = jax.scipy.linalg.block_diag(*[pool_w[l, gi] for gi in range(len(POOL_WINDOWS))])
    w = (w_in[l] * col_scale).astype(BF16)
    return dict(
        norm1_g=row(norm1_g[l]),
        w_in=jnp.concatenate([w[:, :OFF_D], w[:, OFF_D + D_ATT:OFF_G], w[:, OFF_D:OFF_D + D_ATT],
                              w[:, OFF_G:]], axis=1),
        half_b_gate=row(b_gate[l]) * 0.5,
        bias=_attn_bias_table(rpb[l]),
        conv=(conv_a_w[l].astype(F32), row(conv_a_b[l]), row(ln_a_g[l]), row(ln_a_b[l]),
              sc_w[l].astype(F32)),
        mixer=(w_out_a[l].astype(BF16), pool_bd.astype(BF16), row(pool_scale[l]),
               w_out_b[l].astype(BF16), w_out_c[l].astype(BF16), w_out_d[l].astype(BF16),
               w_o[l].astype(BF16)),
        norm2_g=row(norm2_g[l]),
        w_up=(w_up[l] * up_scale).astype(BF16),
        ffn_conv_w=ffn_conv_w[l].astype(F32),
        w_down=w_down[l].astype(BF16),
    )


def _trunk(x3, layers, final_g):
    b, t, d = x3.shape
    x = x3.reshape(b * t, d)
    fg = final_g.reshape(1, -1).astype(F32)
    for li, p in enumerate(layers):
        act, zq, gate = _inproj(x, p["norm1_g"], p["w_in"], p["half_b_gate"], p["conv"], t)
        x = _mixer(zq, p["bias"], act, gate, x, p["mixer"], t)
        x = _ffn(x, p["norm2_g"], p["w_up"], p["ffn_conv_w"], p["w_down"], fg, t,
                 final=li == len(layers) - 1)
    return x.reshape(b, t, d)


def kernel(x_prompt, x_sample, norm1_g, w_in, b_gate, conv_a_w, conv_a_b, ln_a_g, ln_a_b, w_out_a,
           pool_w, pool_scale, w_out_b, sc_w, w_out_c, rpb, w_out_d, w_o, norm2_g, w_up, ffn_conv_w,
           w_down, final_g):
    depth = w_in.shape[0]
    layers = [_prep_layer(l, norm1_g, w_in, b_gate, conv_a_w, conv_a_b, ln_a_g, ln_a_b, w_out_a,
                          pool_w, pool_scale, w_out_b, sc_w, w_out_c, rpb, w_out_d, w_o, norm2_g,
                          w_up, ffn_conv_w, w_down)
              for l in range(depth)]
    return _trunk(x_prompt, layers, final_g), _trunk(x_sample, layers, final_g)
```
